```python
import math
import jax
import jax.numpy as jnp
from jax import lax
import numpy as np

D_MODEL = 4096
BATCH = 4
SEQ = 2048
DEPTH = 4
DEC_BATCH = 128
DEC_SEQ = 8
PAST_LEN = 16384
PAGE_SIZE = 128

N_MIXERS = 3
D_MIX = 3 * D_MODEL // 4
D_XATTN = D_MODEL // 4
D_GATE = D_MIX + D_XATTN
N_MEM = 256
X_HEADS = 4
X_HEAD_DIM = D_XATTN // X_HEADS

HGRN_HEAD_DIM = 128
HGRN_HEADS = D_MIX // HGRN_HEAD_DIM
HGRN_CHUNK = 16

MLSTM_HEADS = 8
MLSTM_DV = D_MIX // MLSTM_HEADS
MLSTM_DQK = MLSTM_DV // 2
MLSTM_QK = MLSTM_HEADS * MLSTM_DQK
MLSTM_CHUNK = 64

RWKV_HEAD_DIM = 64
RWKV_HEADS = D_MIX // RWKV_HEAD_DIM
RWKV_LORA = max(32, int(round(1.8 * math.sqrt(D_MIX) / 32)) * 32)
RWKV_GN_EPS = 64e-5

NORM_EPS = 1e-6

N_HGRN_LAYERS = (DEPTH + N_MIXERS - 1) // N_MIXERS
N_MLSTM_LAYERS = (DEPTH + N_MIXERS - 2) // N_MIXERS
N_RWKV_LAYERS = (DEPTH + N_MIXERS - 3) // N_MIXERS

HGRN_COLS = (D_MIX, D_MIX, D_MIX, D_GATE, D_XATTN)
MLSTM_COLS = (MLSTM_QK, MLSTM_QK, D_MIX, MLSTM_HEADS, MLSTM_HEADS, D_MIX, D_GATE, D_XATTN)
RWKV_COLS = (D_MIX, RWKV_LORA, D_MIX, D_MIX, RWKV_LORA, D_GATE, D_XATTN)

kernel_name = 'hybrid_hgrn2_mlstm_rwkv7_memxattn_step'


def split_cols(a, sizes):
    return jnp.split(a, np.cumsum(sizes)[:-1].tolist(), axis=-1)


def rms_norm(x, gain):
    xf = x.astype(jnp.float32)
    xf = xf * lax.rsqrt(jnp.mean(jnp.square(xf), axis=-1, keepdims=True) + NORM_EPS)
    return (xf * gain.astype(jnp.float32)).astype(x.dtype)


def head_rms_norm(h, gain):
    B, T, H, d = h.shape
    h = h * lax.rsqrt(jnp.mean(jnp.square(h), axis=-1, keepdims=True) + NORM_EPS)
    return h.reshape(B, T, H * d) * gain.astype(jnp.float32)


def to_chunks(a, L):
    B, T, H = a.shape[:3]
    a = a.reshape((B, T // L, L, H) + a.shape[3:])
    return jnp.swapaxes(jnp.moveaxis(a, 1, 0), 2, 3)


def from_chunks(a):
    nC, B, H, L = a.shape[:4]
    a = jnp.moveaxis(jnp.swapaxes(a, 2, 3), 0, 1)
    return a.reshape((B, nC * L, H) + a.shape[4:])


def hgrn_lower_bounds(logits):
    p = jax.nn.softmax(logits.astype(jnp.float32), axis=0)
    return jnp.maximum(jnp.cumsum(p, axis=0) - p[0], 0.0)


def hgrn2_recurrence(q, k, v, log_f, S0):
    T = q.shape[1]
    L = math.gcd(T, HGRN_CHUNK)
    causal = jnp.tril(jnp.ones((L, L), dtype=bool))[:, :, None]

    def step(S, blk):
        qc, kc, vc, gc = blk
        b = jnp.cumsum(gc, axis=2)
        inter = jnp.einsum('bhtd,bhdv->bhtv', qc * jnp.exp(b), S)
        diff = b[:, :, :, None, :] - b[:, :, None, :, :]
        decay = jnp.where(causal, jnp.exp(jnp.where(causal, diff, 0.0)), 0.0)
        attn = jnp.einsum('bhtd,bhsd,bhtsd->bhts', qc, kc, decay)
        intra = jnp.einsum('bhts,bhsv->bhtv', attn, vc)
        b_last = b[:, :, -1:, :]
        S = jnp.exp(b_last[:, :, 0, :])[..., None] * S + jnp.einsum('bhsd,bhsv->bhdv', kc * jnp.exp(b_last - b), vc)
        return S, inter + intra

    S, o = lax.scan(step, S0, (to_chunks(q, L), to_chunks(k, L), to_chunks(v, L), to_chunks(log_f, L)))
    return from_chunks(o), S


def mlstm_recurrence(q, k, v, log_i, log_f, C0, n0, m0):
    T = q.shape[1]
    L = math.gcd(T, MLSTM_CHUNK)
    causal = jnp.tril(jnp.ones((L, L), dtype=bool))

    def step(carry, blk):
        C, n, m = carry
        qc, kc, vc, ic, fc = blk
        b = jnp.cumsum(fc, axis=-1)
        log_w = jnp.where(causal, b[..., :, None] - b[..., None, :] + ic[..., None, :], -jnp.inf)
        log_inter = b + m[..., None]
        m_t = jnp.maximum(log_inter, jnp.max(log_w, axis=-1))
        w = jnp.exp(log_w - m_t[..., None])
        w_inter = jnp.exp(log_inter - m_t)
        a = w * jnp.einsum('bhtd,bhsd->bhts', qc, kc)
        num = w_inter[..., None] * jnp.einsum('bhtd,bhdv->bhtv', qc, C) + jnp.einsum('bhts,bhsv->bhtv', a, vc)
        den = w_inter * jnp.einsum('bhtd,bhd->bht', qc, n) + jnp.sum(a, axis=-1)
        h = num / jnp.maximum(jnp.abs(den), jnp.exp(-m_t))[..., None]
        m_last = m_t[..., -1]
        w_state = jnp.exp(b[..., -1:] - b + ic - m_last[..., None])
        decay = jnp.exp(b[..., -1] + m - m_last)
        C = decay[..., None, None] * C + jnp.einsum('bhs,bhsd,bhsv->bhdv', w_state, kc, vc)
        n = decay[..., None] * n + jnp.einsum('bhs,bhsd->bhd', w_state, kc)
        return (C, n, m_last), h

    blocks = (to_chunks(q, L), to_chunks(k, L), to_chunks(v, L), to_chunks(log_i, L), to_chunks(log_f, L))
    (C, n, m), h = lax.scan(step, (C0, n0, m0), blocks)
    return from_chunks(h), C, n, m


def rwkv7_recurrence(r, w, k, v, a, b, S0):
    def step(S, inp):
        rt, wt, kt, vt, at, bt = inp
        sa = jnp.einsum('bhvk,bhk->bhv', S, at)
        S = S * wt[:, :, None, :] + sa[..., None] * bt[:, :, None, :] + vt[..., None] * kt[:, :, None, :]
        return S, jnp.einsum('bhvk,bhk->bhv', S, rt)

    S, y = lax.scan(step, S0, tuple(jnp.moveaxis(t, 1, 0) for t in (r, w, k, v, a, b)))
    return jnp.moveaxis(y, 0, 1), S


def hgrn2_mixer(h, w_in, lower_bound, norm_g, S0):
    B, T, _ = h.shape
    f32 = jnp.float32
    q, f, i, g, xq = split_cols(jnp.einsum('btd,de->bte', h, w_in), HGRN_COLS)
    f = f.astype(f32)
    lb = lower_bound.astype(f32)
    log_f = jnp.logaddexp(jnp.log(lb), jnp.log1p(-lb) + jax.nn.log_sigmoid(f))
    k = (1.0 - lb) * jax.nn.sigmoid(-f)
    q = jax.nn.silu(q.astype(f32))
    heads = lambda t: t.reshape(B, T, HGRN_HEADS, HGRN_HEAD_DIM)
    o, S = hgrn2_recurrence(heads(q), heads(k), heads(i.astype(f32)), heads(log_f), S0.astype(f32))
    return head_rms_norm(o, norm_g).astype(h.dtype), g, xq, S


def mlstm_mixer(h, w_in, b_gate, norm_g, C0, n0, m0):
    B, T, _ = h.shape
    f32 = jnp.float32
    q, k, v, i_pre, f_pre, o_pre, g, xq = split_cols(jnp.einsum('btd,de->bte', h, w_in), MLSTM_COLS)
    b_gate = b_gate.astype(f32)
    log_i = i_pre.astype(f32) + b_gate[:MLSTM_HEADS]
    log_f = jax.nn.log_sigmoid(f_pre.astype(f32) + b_gate[MLSTM_HEADS:])
    qh = q.astype(f32).reshape(B, T, MLSTM_HEADS, MLSTM_DQK)
    kh = k.astype(f32).reshape(B, T, MLSTM_HEADS, MLSTM_DQK) * (MLSTM_DQK ** -0.5)
    vh = v.astype(f32).reshape(B, T, MLSTM_HEADS, MLSTM_DV)
    hh, C, n, m = mlstm_recurrence(qh, kh, vh, log_i, log_f, C0.astype(f32), n0.astype(f32), m0.astype(f32))
    mix = jax.nn.sigmoid(o_pre.astype(f32)) * head_rms_norm(hh, norm_g)
    return mix.astype(h.dtype), g, xq, C, n, m


def rwkv7_mixer(h, x_prev, w_in, mu, w0, w2, a0, a2, k_k, k_a, r_k, ln_g, ln_b, S0):
    B, T, _ = h.shape
    f32 = lambda t: t.astype(jnp.float32)
    xx = jnp.concatenate([x_prev[:, None, :].astype(h.dtype), h[:, :-1]], axis=1) - h
    w_r, w_w, w_k, w_v, w_a, w_g, w_q = split_cols(w_in, RWKV_COLS)
    lerp = lambda j: h + xx * mu[j]
    r = f32(jnp.einsum('btd,de->bte', lerp(0), w_r))
    w_lo = f32(jnp.einsum('btd,de->bte', lerp(1), w_w))
    k = f32(jnp.einsum('btd,de->bte', lerp(2), w_k))
    v = f32(jnp.einsum('btd,de->bte', lerp(3), w_v))
    a_lo = f32(jnp.einsum('btd,de->bte', lerp(4), w_a))
    g = jnp.einsum('btd,de->bte', lerp(5), w_g)
    xq = jnp.einsum('btd,de->bte', h, w_q)
    log_w = -jax.nn.softplus(-(f32(w0) + jnp.tanh(w_lo) @ f32(w2))) - 0.5
    decay = jnp.exp(-jnp.exp(log_w))
    a = jax.nn.sigmoid(f32(a0) + a_lo @ f32(a2))
    heads = lambda t: t.reshape(B, T, RWKV_HEADS, RWKV_HEAD_DIM)
    kk = heads(k * f32(k_k))
    kk = kk / jnp.maximum(jnp.sqrt(jnp.sum(jnp.square(kk), axis=-1, keepdims=True)), 1e-12)
    k = k * (1.0 + (a - 1.0) * f32(k_a))
    rh, kh, vh = heads(r), heads(k), heads(v)
    y, S = rwkv7_recurrence(rh, heads(decay), kh, vh, -kk, kk * heads(a), S0.astype(jnp.float32))
    mean = jnp.mean(y, axis=-1, keepdims=True)
    var = jnp.mean(jnp.square(y - mean), axis=-1, keepdims=True)
    y = ((y - mean) * lax.rsqrt(var + RWKV_GN_EPS)).reshape(B, T, D_MIX) * f32(ln_g) + f32(ln_b)
    bonus = jnp.sum(rh * kh * f32(r_k), axis=-1, keepdims=True) * vh
    mix = y + bonus.reshape(B, T, D_MIX)
    return mix.astype(h.dtype), g, xq, S, h[:, -1]


def memory_kv(mem, gain, w_kv):
    m = rms_norm(mem, gain)
    k, v = jnp.split(jnp.einsum('bmd,de->bme', m, w_kv), 2, axis=-1)
    B, M = mem.shape[:2]
    return k.reshape(B, M, X_HEADS, X_HEAD_DIM), v.reshape(B, M, X_HEADS, X_HEAD_DIM)


def memory_attend(xq, mem_k, mem_v):
    B, T, _ = xq.shape
    q = xq.reshape(B, T, X_HEADS, X_HEAD_DIM)
    s = jnp.einsum('bthd,bmhd->bhtm', q, mem_k).astype(jnp.float32) * (X_HEAD_DIM ** -0.5)
    p = jax.nn.softmax(s, axis=-1).astype(mem_v.dtype)
    return jnp.einsum('bhtm,bmhd->bthd', p, mem_v).reshape(B, T, D_XATTN)


def trunk(x, mem_k, mem_v, s_hgrn, s_mc, s_mn, s_mm, s_rwkv, s_shift, prm):
    out_hgrn, out_mc, out_mn, out_mm, out_rwkv, out_shift = [], [], [], [], [], []
    for i in range(DEPTH):
        kind, j = i % N_MIXERS, i // N_MIXERS
        h = rms_norm(x, prm['norm_pre'][i])
        if kind == 0:
            mix, g, xq, S = hgrn2_mixer(h, prm['hgrn_w_in'][j], prm['hgrn_lb'][j], prm['hgrn_norm'][j], s_hgrn[j])
            out_hgrn.append(S.astype(s_hgrn.dtype))
        elif kind == 1:
            mix, g, xq, C, n, m = mlstm_mixer(h, prm['mlstm_w_in'][j], prm['mlstm_b_gate'][j], prm['mlstm_norm'][j],
                                              s_mc[j], s_mn[j], s_mm[j])
            out_mc.append(C.astype(s_mc.dtype))
            out_mn.append(n.astype(s_mn.dtype))
            out_mm.append(m.astype(s_mm.dtype))
        else:
            mix, g, xq, S, last = rwkv7_mixer(h, s_shift[j], prm['rwkv_w_in'][j], prm['rwkv_mu'][j], prm['rwkv_w0'][j],
                                              prm['rwkv_w2'][j], prm['rwkv_a0'][j], prm['rwkv_a2'][j], prm['rwkv_k_k'][j],
                                              prm['rwkv_k_a'][j], prm['rwkv_r_k'][j], prm['rwkv_ln_g'][j],
                                              prm['rwkv_ln_b'][j], s_rwkv[j])
            out_rwkv.append(S.astype(s_rwkv.dtype))
            out_shift.append(last.astype(s_shift.dtype))
        xo = memory_attend(xq, mem_k[i], mem_v[i])
        u = jnp.concatenate([mix, xo.astype(mix.dtype)], axis=-1) * jax.nn.silu(g)
        y = jnp.einsum('bte,ed->btd', u, prm['w_out'][i])
        x = x + rms_norm(y, prm['norm_post'][i])
    return (x, jnp.stack(out_hgrn), jnp.stack(out_mc), jnp.stack(out_mn), jnp.stack(out_mm),
            jnp.stack(out_rwkv), jnp.stack(out_shift))


def setup_inputs(seed: int = 0) -> dict:
    key = jax.random.key(seed)
    ks = iter(jax.random.split(key, 48))
    nrm = lambda shape, scale=1.0: scale * jax.random.normal(next(ks), shape, jnp.float32)
    gain = lambda shape: 1.0 + nrm(shape, 0.05)
    return {
        'x_prompt': nrm((BATCH, SEQ, D_MODEL)),
        'x_sample': nrm((DEC_BATCH, DEC_SEQ, D_MODEL)),
        'mem_prompt': nrm((BATCH, N_MEM, D_MODEL)),
        'cache_mem_k': nrm((DEPTH, DEC_BATCH, N_MEM, X_HEADS, X_HEAD_DIM)),
        'cache_mem_v': nrm((DEPTH, DEC_BATCH, N_MEM, X_HEADS, X_HEAD_DIM)),
        'state_hgrn': nrm((N_HGRN_LAYERS, DEC_BATCH, HGRN_HEADS, HGRN_HEAD_DIM, HGRN_HEAD_DIM), 0.5),
        'state_mlstm_c': nrm((N_MLSTM_LAYERS, DEC_BATCH, MLSTM_HEADS, MLSTM_DQK, MLSTM_DV), 0.5),
        'state_mlstm_n': nrm((N_MLSTM_LAYERS, DEC_BATCH, MLSTM_HEADS, MLSTM_DQK), 0.5),
        'state_mlstm_m': nrm((N_MLSTM_LAYERS, DEC_BATCH, MLSTM_HEADS)),
        'state_rwkv': nrm((N_RWKV_LAYERS, DEC_BATCH, RWKV_HEADS, RWKV_HEAD_DIM, RWKV_HEAD_DIM), 0.3),
        'state_rwkv_shift': nrm((N_RWKV_LAYERS, DEC_BATCH, D_MODEL)),
        'norm_pre': gain((DEPTH, D_MODEL)),
        'norm_post': gain((DEPTH, D_MODEL)),
        'norm_mem': gain((DEPTH, D_MODEL)),
        'w_mem_kv': nrm((DEPTH, D_MODEL, 2 * D_XATTN), D_MODEL ** -0.5),
        'w_out': nrm((DEPTH, D_GATE, D_MODEL), D_GATE ** -0.5),
        'hgrn_w_in': nrm((N_HGRN_LAYERS, D_MODEL, sum(HGRN_COLS)), D_MODEL ** -0.5),
        'hgrn_lb_logits': nrm((N_HGRN_LAYERS, D_MIX), 0.5),
        'hgrn_norm': gain((N_HGRN_LAYERS, D_MIX)),
        'mlstm_w_in': nrm((N_MLSTM_LAYERS, D_MODEL, sum(MLSTM_COLS)), D_MODEL ** -0.5),
        'mlstm_b_gate': nrm((N_MLSTM_LAYERS, 2 * MLSTM_HEADS), 0.1),
        'mlstm_norm': gain((N_MLSTM_LAYERS, D_MIX)),
        'rwkv_w_in': nrm((N_RWKV_LAYERS, D_MODEL, sum(RWKV_COLS)), D_MODEL ** -0.5),
        'rwkv_mu': jax.random.uniform(next(ks), (N_RWKV_LAYERS, 6, D_MODEL), jnp.float32),
        'rwkv_w0': nrm((N_RWKV_LAYERS, D_MIX), 0.5),
        'rwkv_w2': nrm((N_RWKV_LAYERS, RWKV_LORA, D_MIX), 0.5 * RWKV_LORA ** -0.5),
        'rwkv_a0': nrm((N_RWKV_LAYERS, D_MIX), 0.1),
        'rwkv_a2': nrm((N_RWKV_LAYERS, RWKV_LORA, D_MIX), RWKV_LORA ** -0.5),
        'rwkv_k_k': 1.0 + nrm((N_RWKV_LAYERS, D_MIX), 0.1),
        'rwkv_k_a': 1.0 + nrm((N_RWKV_LAYERS, D_MIX), 0.1),
        'rwkv_r_k': nrm((N_RWKV_LAYERS, RWKV_HEADS, RWKV_HEAD_DIM), 0.1),
        'rwkv_ln_g': gain((N_RWKV_LAYERS, D_MIX)),
        'rwkv_ln_b': nrm((N_RWKV_LAYERS, D_MIX), 0.02),
    }


def reference(x_prompt, x_sample, mem_prompt, cache_mem_k, cache_mem_v,
              state_hgrn, state_mlstm_c, state_mlstm_n, state_mlstm_m, state_rwkv, state_rwkv_shift,
              norm_pre, norm_post, norm_mem, w_mem_kv, w_out,
              hgrn_w_in, hgrn_lb_logits, hgrn_norm,
              mlstm_w_in, mlstm_b_gate, mlstm_norm,
              rwkv_w_in, rwkv_mu, rwkv_w0, rwkv_w2, rwkv_a0, rwkv_a2,
              rwkv_k_k, rwkv_k_a, rwkv_r_k, rwkv_ln_g, rwkv_ln_b):
    prm = dict(norm_pre=norm_pre, norm_post=norm_post, w_out=w_out,
               hgrn_w_in=hgrn_w_in, hgrn_lb=hgrn_lower_bounds(hgrn_lb_logits), hgrn_norm=hgrn_norm,
               mlstm_w_in=mlstm_w_in, mlstm_b_gate=mlstm_b_gate, mlstm_norm=mlstm_norm,
               rwkv_w_in=rwkv_w_in, rwkv_mu=rwkv_mu, rwkv_w0=rwkv_w0, rwkv_w2=rwkv_w2,
               rwkv_a0=rwkv_a0, rwkv_a2=rwkv_a2, rwkv_k_k=rwkv_k_k, rwkv_k_a=rwkv_k_a,
               rwkv_r_k=rwkv_r_k, rwkv_ln_g=rwkv_ln_g, rwkv_ln_b=rwkv_ln_b)

    mem_kv = [memory_kv(mem_prompt, norm_mem[i], w_mem_kv[i]) for i in range(DEPTH)]
    mem_k_prompt = jnp.stack([kv[0] for kv in mem_kv])
    mem_v_prompt = jnp.stack([kv[1] for kv in mem_kv])
    B, dt = x_prompt.shape[0], x_prompt.dtype
    zeros = lambda *shape: jnp.zeros(shape, dt)
    (y_prompt, hgrn_prompt, mlstm_c_prompt, mlstm_n_prompt, mlstm_m_prompt,
     rwkv_prompt, rwkv_shift_prompt) = trunk(
        x_prompt, mem_k_prompt, mem_v_prompt,
        zeros(N_HGRN_LAYERS, B, HGRN_HEADS, HGRN_HEAD_DIM, HGRN_HEAD_DIM),
        zeros(N_MLSTM_LAYERS, B, MLSTM_HEADS, MLSTM_DQK, MLSTM_DV),
        zeros(N_MLSTM_LAYERS, B, MLSTM_HEADS, MLSTM_DQK),
        zeros(N_MLSTM_LAYERS, B, MLSTM_HEADS),
        zeros(N_RWKV_LAYERS, B, RWKV_HEADS, RWKV_HEAD_DIM, RWKV_HEAD_DIM),
        zeros(N_RWKV_LAYERS, B, D_MODEL),
        prm)

    (y_sample, hgrn_sample, mlstm_c_sample, mlstm_n_sample, mlstm_m_sample,
     rwkv_sample, rwkv_shift_sample) = trunk(
        x_sample, cache_mem_k, cache_mem_v, state_hgrn, state_mlstm_c, state_mlstm_n, state_mlstm_m,
        state_rwkv, state_rwkv_shift, prm)

    return (y_prompt, y_sample, mem_k_prompt, mem_v_prompt,
            hgrn_prompt, mlstm_c_prompt, mlstm_n_prompt, mlstm_m_prompt, rwkv_prompt, rwkv_shift_prompt,
            hgrn_sample, mlstm_c_sample, mlstm_n_sample, mlstm_m_sample, rwkv_sample, rwkv_shift_sample)
```

```python
import functools
import math

import jax
import jax.numpy as jnp
from jax import lax
from jax.experimental import pallas as pl
from jax.experimental.pallas import tpu as pltpu

F32 = jnp.float32
BF16 = jnp.bfloat16

D_MODEL = 4096
DEPTH = 4
N_MIXERS = 3
D_MIX = 3 * D_MODEL // 4
D_XATTN = D_MODEL // 4
D_GATE = D_MIX + D_XATTN
N_MEM = 256
X_HEADS = 4
X_HEAD_DIM = D_XATTN // X_HEADS

HGRN_HEAD_DIM = 128
HGRN_HEADS = D_MIX // HGRN_HEAD_DIM
HGRN_CHUNK = 16

MLSTM_HEADS = 8
MLSTM_DV = D_MIX // MLSTM_HEADS
MLSTM_DQK = MLSTM_DV // 2
MLSTM_DQK_PAD = 256
MLSTM_CHUNK = 64

RWKV_HEAD_DIM = 64
RWKV_HEADS = D_MIX // RWKV_HEAD_DIM
RWKV_PAIRS = RWKV_HEADS // 2
RWKV_LORA = max(32, int(round(1.8 * math.sqrt(D_MIX) / 32)) * 32)
RWKV_LORA_PAD = 128
RWKV_GN_EPS = 64e-5
RWKV_CHUNK = 16

NORM_EPS = 1e-6
LANES = 128
VMEM_LIMIT = 48 * 1024 * 1024


def _cparams(*sem):
    return pltpu.CompilerParams(dimension_semantics=sem, vmem_limit_bytes=VMEM_LIMIT)


def _mm(a, b, **kw):
    return pl.dot(a.astype(BF16), b.astype(BF16), **kw)


def _sigmoid(x):
    return 1.0 / (1.0 + jnp.exp(-x))


def _log_sigmoid(x):
    return jnp.minimum(x, 0.0) - jnp.log1p(jnp.exp(-jnp.abs(x)))


def _cumsum_rows(x, n_rows):
    row = lax.broadcasted_iota(jnp.int32, x.shape, 0)
    sh = 1
    while sh < n_rows:
        x = x + jnp.where(row >= sh, pltpu.roll(x, sh, axis=0), 0.0)
        sh *= 2
    return x


def _rmsnorm_kernel(x_ref, g_ref, o_ref):
    x = x_ref[...]
    ms = jnp.mean(x * x, axis=-1, keepdims=True)
    o_ref[...] = ((x * lax.rsqrt(ms + NORM_EPS)) * g_ref[...]).astype(o_ref.dtype)


def rmsnorm(x, gain, out_dtype, tm=256):
    m, d = x.shape
    tm = min(tm, m)
    return pl.pallas_call(
        _rmsnorm_kernel,
        grid=(m // tm,),
        in_specs=[pl.BlockSpec((tm, d), lambda i: (i, 0)), pl.BlockSpec((1, d), lambda i: (0, 0))],
        out_specs=pl.BlockSpec((tm, d), lambda i: (i, 0)),
        out_shape=jax.ShapeDtypeStruct((m, d), out_dtype),
        compiler_params=_cparams("parallel"),
        name="rmsnorm",
    )(x, gain.reshape(1, d))


def _mm_kernel(a_ref, w_ref, o_ref):
    @pl.when(pl.program_id(2) == 0)
    def _():
        o_ref[...] = jnp.zeros_like(o_ref)

    o_ref[...] += _mm(a_ref[...], w_ref[...])


def matmul(a, w, tm=1024, tn=1024, tk=512):
    m, k = a.shape
    n = w.shape[1]
    tm, tn, tk = min(tm, m), min(tn, n), min(tk, k)
    return pl.pallas_call(
        _mm_kernel,
        grid=(m // tm, n // tn, k // tk),
        in_specs=[pl.BlockSpec((tm, tk), lambda i, j, kk: (i, kk)),
                  pl.BlockSpec((tk, tn), lambda i, j, kk: (kk, j))],
        out_specs=pl.BlockSpec((tm, tn), lambda i, j, kk: (i, j)),
        out_shape=jax.ShapeDtypeStruct((m, n), F32),
        compiler_params=_cparams("parallel", "parallel", "arbitrary"),
        name="matmul",
    )(a, w)


def _mm_lerp_kernel(h_ref, hp_ref, mu_ref, w_ref, o_ref):
    @pl.when(pl.program_id(2) == 0)
    def _():
        o_ref[...] = jnp.zeros_like(o_ref)

    h = h_ref[...]
    lhs = h + (hp_ref[...] - h) * mu_ref[0]
    o_ref[...] += _mm(lhs, w_ref[...])


def matmul_lerp(h, hp, mu_tiles, w, tm=1024, tn=1024, tk=512):
    m, k = h.shape
    n = w.shape[1]
    tm, tn, tk = min(tm, m), min(tn, n), min(tk, k)
    assert mu_tiles.shape == (n // tn, 1, k)
    return pl.pallas_call(
        _mm_lerp_kernel,
        grid=(m // tm, n // tn, k // tk),
        in_specs=[pl.BlockSpec((tm, tk), lambda i, j, kk: (i, kk)),
                  pl.BlockSpec((tm, tk), lambda i, j, kk: (i, kk)),
                  pl.BlockSpec((1, 1, tk), lambda i, j, kk: (j, 0, kk)),
                  pl.BlockSpec((tk, tn), lambda i, j, kk: (kk, j))],
        out_specs=pl.BlockSpec((tm, tn), lambda i, j, kk: (i, j)),
        out_shape=jax.ShapeDtypeStruct((m, n), F32),
        compiler_params=_cparams("parallel", "parallel", "arbitrary"),
        name="matmul_lerp",
    )(h, hp, mu_tiles, w)


def _outproj_kernel(mix_ref, xo_ref, g_ref, w_ref, x_ref, gain_ref, o_ref, acc_ref, *, k_mix):
    kk = pl.program_id(1)

    @pl.when(kk == 0)
    def _():
        acc_ref[...] = jnp.zeros_like(acc_ref)

    g = g_ref[...]
    u = jnp.where(kk < k_mix, mix_ref[...], xo_ref[...]) * (g * _sigmoid(g))
    acc_ref[...] += _mm(u, w_ref[...])

    @pl.when(kk == pl.num_programs(1) - 1)
    def _():
        y = acc_ref[...]
        ms = jnp.mean(y * y, axis=-1, keepdims=True)
        o_ref[...] = x_ref[...] + (y * lax.rsqrt(ms + NORM_EPS)) * gain_ref[...]


def outproj(mix, xo, p, gate_col, w_out, x, gain, tm=256, tk=512):
    m, d_mix = mix.shape
    d_x = xo.shape[1]
    kdim, n = w_out.shape
    tm = min(tm, m)
    assert d_mix % tk == 0 and d_x % tk == 0 and gate_col % tk == 0 and kdim == d_mix + d_x
    k_mix, g_off = d_mix // tk, gate_col // tk
    return pl.pallas_call(
        functools.partial(_outproj_kernel, k_mix=k_mix),
        grid=(m // tm, kdim // tk),
        in_specs=[pl.BlockSpec((tm, tk), lambda i, kk: (i, jnp.minimum(kk, k_mix - 1))),
                  pl.BlockSpec((tm, tk), lambda i, kk: (i, jnp.maximum(kk - k_mix, 0))),
                  pl.BlockSpec((tm, tk), lambda i, kk: (i, g_off + kk)),
                  pl.BlockSpec((tk, n), lambda i, kk: (kk, 0)),
                  pl.BlockSpec((tm, n), lambda i, kk: (i, 0)),
                  pl.BlockSpec((1, n), lambda i, kk: (0, 0))],
        out_specs=pl.BlockSpec((tm, n), lambda i, kk: (i, 0)),
        out_shape=jax.ShapeDtypeStruct((m, n), F32),
        scratch_shapes=[pltpu.VMEM((tm, n), F32)],
        compiler_params=_cparams("parallel", "arbitrary"),
        name="outproj",
    )(mix, xo, p, w_out, x, gain.reshape(1, n))


def _memattn_kernel(q_ref, k_ref, v_ref, o_ref, *, heads, hd):
    scale = hd ** -0.5
    for h in range(heads):
        cs = slice(h * hd, (h + 1) * hd)
        s = _mm(q_ref[0, :, cs], k_ref[0, :, cs], trans_b=True) * scale
        s = s - jnp.max(s, axis=-1, keepdims=True)
        e = jnp.exp(s)
        p = e / jnp.sum(e, axis=-1, keepdims=True)
        o_ref[0, :, cs] = _mm(p, v_ref[0, :, cs])


def memattn(p, q_col, mem_k, mem_v, b, t, tq=512):
    heads, hd = mem_k.shape[2], mem_k.shape[3]
    n_mem, w = mem_k.shape[1], heads * hd
    tq = min(tq, t)
    assert q_col % w == 0
    q_off = q_col // w
    p3 = p.reshape(b, t, p.shape[1])
    out = pl.pallas_call(
        functools.partial(_memattn_kernel, heads=heads, hd=hd),
        grid=(b, t // tq),
        in_specs=[pl.BlockSpec((1, tq, w), lambda i, j: (i, j, q_off)),
                  pl.BlockSpec((1, n_mem, w), lambda i, j: (i, 0, 0)),
                  pl.BlockSpec((1, n_mem, w), lambda i, j: (i, 0, 0))],
        out_specs=pl.BlockSpec((1, tq, w), lambda i, j: (i, j, 0)),
        out_shape=jax.ShapeDtypeStruct((b, t, w), F32),
        compiler_params=_cparams("parallel", "parallel"),
        name="memattn",
    )(p3, mem_k.reshape(b, n_mem, w), mem_v.reshape(b, n_mem, w))
    return out.reshape(b * t, w)


def _hgrn_kernel(q_ref, f_ref, i_ref, loglb_ref, l1m_ref, oml_ref, g_ref, s0_ref,
                 o_ref, sout_ref, st_ref, *, L, Hb, nC):
    hd = HGRN_HEAD_DIM
    for hh in range(Hb):
        st_ref[hh] = s0_ref[0, hh].T
    row = lax.broadcasted_iota(jnp.int32, (L, hd), 0)

    def chunk(c, carry):
        r0 = pl.multiple_of(c * L, L)
        for hh in range(Hb):
            cs = slice(hh * hd, (hh + 1) * hd)
            q = q_ref[0, pl.ds(r0, L), cs]
            f = f_ref[0, pl.ds(r0, L), cs]
            vh = i_ref[0, pl.ds(r0, L), cs]
            qh = q * _sigmoid(q)
            a = loglb_ref[:, cs]
            cc = l1m_ref[:, cs] + _log_sigmoid(f)
            lf = jnp.maximum(a, cc) + jnp.log1p(jnp.exp(-jnp.abs(a - cc)))
            kh = oml_ref[:, cs] * _sigmoid(-f)
            b = _cumsum_rows(lf, L)
            st = st_ref[hh]
            inter = _mm(qh * jnp.exp(b), st, trans_b=True)
            intra = jnp.zeros((L, hd), F32)
            for t in range(L):
                msk = row <= t
                dec = jnp.where(msk, jnp.exp(jnp.where(msk, b[t:t + 1, :] - b, 0.0)), 0.0)
                at = jnp.sum((qh[t:t + 1, :] * kh) * dec, axis=-1, keepdims=True)
                it = jnp.sum(at * vh, axis=0, keepdims=True)
                intra = jnp.where(row == t, it, intra)
            o = inter + intra
            o = o * lax.rsqrt(jnp.mean(o * o, axis=-1, keepdims=True) + NORM_EPS)
            o_ref[0, pl.ds(r0, L), cs] = o * g_ref[:, cs]
            b_last = b[L - 1:L, :]
            kd = kh * jnp.exp(b_last - b)
            st_ref[hh] = st * jnp.exp(b_last) + _mm(vh, kd, trans_a=True)
        return carry

    lax.fori_loop(0, nC, chunk, 0)
    for hh in range(Hb):
        sout_ref[0, hh] = st_ref[hh].T


def hgrn_mix(p, b, t, lb, norm_g, s0, hb):
    heads, hd = s0.shape[1], s0.shape[2]
    dm = heads * hd
    L = math.gcd(t, HGRN_CHUNK)
    w = hb * hd
    nhb = heads // hb
    p3 = p.reshape(b, t, p.shape[1])
    row = lambda a: a.reshape(1, dm).astype(F32)
    par_spec = pl.BlockSpec((1, w), lambda i, j: (0, j))
    st_spec = pl.BlockSpec((1, hb, hd, hd), lambda i, j: (i, j, 0, 0))
    mix, s_out = pl.pallas_call(
        functools.partial(_hgrn_kernel, L=L, Hb=hb, nC=t // L),
        grid=(b, nhb),
        in_specs=[pl.BlockSpec((1, t, w), lambda i, j: (i, 0, j)),
                  pl.BlockSpec((1, t, w), lambda i, j: (i, 0, nhb + j)),
                  pl.BlockSpec((1, t, w), lambda i, j: (i, 0, 2 * nhb + j)),
                  par_spec, par_spec, par_spec, par_spec, st_spec],
        out_specs=[pl.BlockSpec((1, t, w), lambda i, j: (i, 0, j)), st_spec],
        out_shape=[jax.ShapeDtypeStruct((b, t, dm), F32), jax.ShapeDtypeStruct(s0.shape, F32)],
        scratch_shapes=[pltpu.VMEM((hb, hd, hd), F32)],
        compiler_params=_cparams("parallel", "parallel"),
        name="hgrn_mix",
    )(p3, p3, p3, row(jnp.log(lb)), row(jnp.log1p(-lb)), row(1.0 - lb), row(norm_g), s0)
    return mix.reshape(b * t, dm), s_out


def _mlstm_kernel(bg_ref, q_ref, k_ref, v_ref, op_ref, ig_ref, fg_ref, g_ref, c0_ref, n0_ref, m0_ref,
                  o_ref, cout_ref, nout_ref, mout_ref, c_sc, n_sc, m_sc, *, L, nC, heads, dqk):
    h = pl.program_id(1)
    b_i = bg_ref[h]
    b_f = bg_ref[heads + h]
    c_sc[...] = jnp.zeros_like(c_sc)
    c_sc[0:dqk, :] = c0_ref[0, 0]
    n_sc[...] = jnp.zeros_like(n_sc)
    n_sc[:, 0:dqk] = n0_ref[0, 0]
    m_sc[...] = m0_ref[0, 0]
    ti = lax.broadcasted_iota(jnp.int32, (L, L), 0)
    si = lax.broadcasted_iota(jnp.int32, (L, L), 1)
    eye, low = ti == si, si <= ti
    col = lambda r: jnp.sum(jnp.where(eye, r, 0.0), axis=1, keepdims=True)
    kscale = dqk ** -0.5

    def chunk(c, carry):
        r0 = pl.multiple_of(c * L, L)
        q = q_ref[0, pl.ds(r0, L), :]
        k = k_ref[0, pl.ds(r0, L), :] * kscale
        v = v_ref[0, pl.ds(r0, L), :]
        ic_row = ig_ref[0, 0, pl.ds(c, 1), :] + b_i
        lf_row = _log_sigmoid(fg_ref[0, 0, pl.ds(c, 1), :] + b_f)
        b_col = jnp.sum(jnp.where(low, lf_row, 0.0), axis=1, keepdims=True)
        b_row = jnp.sum(jnp.where(ti <= si, col(lf_row), 0.0), axis=0, keepdims=True)
        log_w = jnp.where(low, b_col - b_row + ic_row, -jnp.inf)
        m = m_sc[...]
        log_inter = b_col + m
        m_t = jnp.maximum(log_inter, jnp.max(log_w, axis=1, keepdims=True))
        w = jnp.exp(log_w - m_t)
        w_inter = jnp.exp(log_inter - m_t)
        a = w * _mm(q, k, trans_b=True)
        cst, n = c_sc[...], n_sc[...]
        num = w_inter * _mm(q, cst) + _mm(a, v)
        den = w_inter * jnp.sum(q * n, axis=1, keepdims=True) + jnp.sum(a, axis=1, keepdims=True)
        hh = num / jnp.maximum(jnp.abs(den), jnp.exp(-m_t))
        m_last = m_t[L - 1:L, :]
        b_last = b_col[L - 1:L, :]
        w_state = jnp.exp(b_last - b_col + col(ic_row) - m_last)
        decay = jnp.exp(b_last + m - m_last)
        kw = w_state * k
        c_sc[...] = decay * cst + _mm(kw, v, trans_a=True)
        n_sc[...] = decay * n + jnp.sum(kw, axis=0, keepdims=True)
        m_sc[...] = m_last
        hn = hh * lax.rsqrt(jnp.mean(hh * hh, axis=1, keepdims=True) + NORM_EPS) * g_ref[...]
        o_ref[0, pl.ds(r0, L), :] = _sigmoid(op_ref[0, pl.ds(r0, L), :]) * hn
        return carry

    lax.fori_loop(0, nC, chunk, 0)
    cout_ref[0, 0] = c_sc[0:dqk, :]
    nout_ref[0, 0] = n_sc[:, 0:dqk]
    mout_ref[0, 0] = m_sc[...]


def mlstm_mix(p, gates, b, t, b_gate, norm_g, c0, n0, m0):
    heads, dqk, dv = c0.shape[1], c0.shape[2], c0.shape[3]
    dqp = MLSTM_DQK_PAD
    dm = heads * dv
    L = math.gcd(t, MLSTM_CHUNK)
    nC = t // L
    p3 = p.reshape(b, t, p.shape[1])
    g3 = gates.reshape(b, t, 2 * heads)
    ig = jnp.swapaxes(g3[..., :heads], 1, 2).reshape(b, heads, nC, L)
    fg = jnp.swapaxes(g3[..., heads:], 1, 2).reshape(b, heads, nC, L)
    q_off, k_off = 2 * dm // dqp, 2 * dm // dqp + heads
    gate_spec = pl.BlockSpec((1, 1, nC, L), lambda i, j: (i, j, 0, 0))
    c_spec = pl.BlockSpec((1, 1, dqk, dv), lambda i, j: (i, j, 0, 0))
    n_spec = pl.BlockSpec((1, 1, 1, dqk), lambda i, j: (i, j, 0, 0))
    m_spec = pl.BlockSpec((1, 1, 1, 1), lambda i, j: (i, j, 0, 0))
    mix, c, n, m = pl.pallas_call(
        functools.partial(_mlstm_kernel, L=L, nC=nC, heads=heads, dqk=dqk),
        grid=(b, heads),
        in_specs=[pl.BlockSpec(memory_space=pltpu.SMEM),
                  pl.BlockSpec((1, t, dqp), lambda i, j: (i, 0, q_off + j)),
                  pl.BlockSpec((1, t, dqp), lambda i, j: (i, 0, k_off + j)),
                  pl.BlockSpec((1, t, dv), lambda i, j: (i, 0, j)),
                  pl.BlockSpec((1, t, dv), lambda i, j: (i, 0, heads + j)),
                  gate_spec, gate_spec,
                  pl.BlockSpec((1, dv), lambda i, j: (0, j)),
                  c_spec, n_spec, m_spec],
        out_specs=[pl.BlockSpec((1, t, dv), lambda i, j: (i, 0, j)), c_spec, n_spec, m_spec],
        out_shape=[jax.ShapeDtypeStruct((b, t, dm), F32),
                   jax.ShapeDtypeStruct((b, heads, dqk, dv), F32),
                   jax.ShapeDtypeStruct((b, heads, 1, dqk), F32),
                   jax.ShapeDtypeStruct((b, heads, 1, 1), F32)],
        scratch_shapes=[pltpu.VMEM((dqp, dv), F32), pltpu.VMEM((1, dqp), F32), pltpu.VMEM((1, 1), F32)],
        compiler_params=_cparams("parallel", "parallel"),
        name="mlstm_mix",
    )(b_gate.astype(F32), p3, p3, p3, p3, ig, fg, norm_g.reshape(1, dm).astype(F32),
      c0, n0.reshape(b, heads, 1, dqk), m0.reshape(b, heads, 1, 1))
    return mix.reshape(b * t, dm), c, n.reshape(b, heads, dqk), m.reshape(b, heads)


def _seg_sum(x, lane_lo):
    s0 = jnp.sum(jnp.where(lane_lo, x, 0.0), axis=-1, keepdims=True)
    s1 = jnp.sum(jnp.where(lane_lo, 0.0, x), axis=-1, keepdims=True)
    return jnp.where(lane_lo, s0, s1)


def _rwkv_prep_kernel(k_ref, wlo_ref, alo_ref, w2_ref, a2_ref, w0_ref, a0_ref, kk_ref, ka_ref,
                      ld_ref, km_ref, av_ref, bv_ref):
    k = k_ref[...]
    lw = w0_ref[...] + _mm(jnp.tanh(wlo_ref[...]), w2_ref[...])
    log_w = -(jnp.maximum(-lw, 0.0) + jnp.log1p(jnp.exp(-jnp.abs(lw)))) - 0.5
    ld_ref[...] = -jnp.exp(log_w)
    a = _sigmoid(a0_ref[...] + _mm(alo_ref[...], a2_ref[...]))
    kk = k * kk_ref[...]
    tn = k.shape[1]
    lane_lo = lax.broadcasted_iota(jnp.int32, (k.shape[0], LANES), 1) < RWKV_HEAD_DIM
    for c in range(tn // LANES):
        cs = slice(c * LANES, (c + 1) * LANES)
        kc = kk[:, cs]
        kc = kc / jnp.maximum(jnp.sqrt(_seg_sum(kc * kc, lane_lo)), 1e-12)
        av_ref[:, cs] = -kc
        bv_ref[:, cs] = kc * a[:, cs]
    km_ref[...] = k * (1.0 + (a - 1.0) * ka_ref[...])


def rwkv_prep(p, k_col, lora, w2, a2, w0, a0, k_k, k_a, tm=512, tn=512):
    m = p.shape[0]
    dm = w2.shape[1]
    lp = RWKV_LORA_PAD
    tm = min(tm, m)
    k_off = k_col // tn
    par = pl.BlockSpec((1, tn), lambda i, j: (0, j))
    out = pl.BlockSpec((tm, tn), lambda i, j: (i, j))
    row = lambda a: a.reshape(1, dm).astype(F32)
    return pl.pallas_call(
        _rwkv_prep_kernel,
        grid=(m // tm, dm // tn),
        in_specs=[pl.BlockSpec((tm, tn), lambda i, j: (i, k_off + j)),
                  pl.BlockSpec((tm, lp), lambda i, j: (i, 0)),
                  pl.BlockSpec((tm, lp), lambda i, j: (i, 1)),
                  pl.BlockSpec((lp, tn), lambda i, j: (0, j)),
                  pl.BlockSpec((lp, tn), lambda i, j: (0, j)),
                  par, par, par, par],
        out_specs=[out, out, out, out],
        out_shape=[jax.ShapeDtypeStruct((m, dm), F32)] * 4,
        compiler_params=_cparams("parallel", "parallel"),
        name="rwkv_prep",
    )(p, lora, lora, w2, a2, row(w0), row(a0), row(k_k), row(k_a))


def _rwkv_kernel(r_ref, ld_ref, k_ref, v_ref, a_ref, b_ref, rk_ref, lg_ref, lb_ref, s0_ref,
                 y_ref, sout_ref, sp_ref, *, L, Pb, nC):
    hd = RWKV_HEAD_DIM
    bm = (lax.broadcasted_iota(jnp.int32, (LANES, LANES), 0) // hd
          == lax.broadcasted_iota(jnp.int32, (LANES, LANES), 1) // hd)
    for p in range(Pb):
        x = s0_ref[0, p]
        sp_ref[p] = jnp.where(bm, jnp.concatenate([x, x], axis=1), 0.0)
    lo1 = lax.broadcasted_iota(jnp.int32, (L, LANES), 1) < hd
    lo2 = lax.broadcasted_iota(jnp.int32, (2 * L, LANES), 1) < hd
    ti = lax.broadcasted_iota(jnp.int32, (L, L), 0)
    si = lax.broadcasted_iota(jnp.int32, (L, L), 1)
    n_dbl = max(1, (L - 1).bit_length())

    def chunk(c, carry):
        r0 = pl.multiple_of(c * L, L)
        for p in range(Pb):
            cs = slice(p * LANES, (p + 1) * LANES)
            ld = lambda ref: ref[0, pl.ds(r0, L), cs]
            r, ldec, k, v, av, bv = ld(r_ref), ld(ld_ref), ld(k_ref), ld(v_ref), ld(a_ref), ld(b_ref)
            lc = _cumsum_rows(ldec, L)
            lcl = lc[L - 1:L, :]
            e_neg = jnp.exp(-lc)
            e_rem = jnp.exp(lcl - lc)
            ar = jnp.concatenate([av * jnp.exp(lc - ldec), r * jnp.exp(lc)], axis=0)
            bt, kt = bv * e_neg, k * e_neg
            sp = sp_ref[p]
            ur0 = _mm(ar, sp, trans_b=True)
            u0, y0 = ur0[:L], ur0[L:]
            mats = []
            u = None
            for j in range(2):
                arj = jnp.where(lo2 if j == 0 else ~lo2, ar, 0.0)
                gb = _mm(arj, bt, trans_b=True)
                gk = _mm(arj, kt, trans_b=True)
                a_ab = jnp.where(si < ti, gb[:L], 0.0)
                a_ak = jnp.where(si < ti, gk[:L], 0.0)
                mats.append((jnp.where(si <= ti, gb[L:], 0.0), jnp.where(si <= ti, gk[L:], 0.0)))
                x = u0 + _mm(a_ak, v)
                nmat = a_ab
                for it in range(n_dbl):
                    x = x + _mm(nmat, x)
                    if it + 1 < n_dbl:
                        nmat = _mm(nmat, nmat)
                u = x if j == 0 else jnp.where(lo1, u, x)
            ys = [y0 + _mm(a_rb, u) + _mm(a_rk, v) for a_rb, a_rk in mats]
            y = jnp.where(lo1, ys[0], ys[1])
            upd = _mm(jnp.concatenate([u, v], axis=0),
                      jnp.concatenate([bv * e_rem, k * e_rem], axis=0), trans_a=True)
            sp_ref[p] = sp * jnp.exp(lcl) + jnp.where(bm, upd, 0.0)
            mean = _seg_sum(y, lo1) * (1.0 / hd)
            yc = y - mean
            var = _seg_sum(yc * yc, lo1) * (1.0 / hd)
            yn = yc * lax.rsqrt(var + RWKV_GN_EPS) * lg_ref[:, cs] + lb_ref[:, cs]
            bonus = _seg_sum(r * k * rk_ref[:, cs], lo1) * v
            y_ref[0, pl.ds(r0, L), cs] = yn + bonus
        return carry

    lax.fori_loop(0, nC, chunk, 0)
    row_lo = lax.broadcasted_iota(jnp.int32, (LANES, hd), 0) < hd
    for p in range(Pb):
        sp = sp_ref[p]
        sout_ref[0, p] = jnp.where(row_lo, sp[:, :hd], sp[:, hd:])


def rwkv_mix(r_src, r_col, v_col, ld, km, av, bv, b, t, r_k, ln_g, ln_b, s0, pb):
    heads, hd = s0.shape[1], s0.shape[2]
    dm = heads * hd
    pairs = heads // 2
    L = math.gcd(t, RWKV_CHUNK)
    w = pb * LANES
    npb = pairs // pb
    r_off, v_off = r_col // w, v_col // w
    three = lambda a: a.reshape(b, t, a.shape[1])
    row = lambda a: a.reshape(1, dm).astype(F32)
    seq = lambda off: pl.BlockSpec((1, t, w), lambda i, j: (i, 0, off + j))
    par = pl.BlockSpec((1, w), lambda i, j: (0, j))
    st_spec = pl.BlockSpec((1, pb, LANES, hd), lambda i, j: (i, j, 0, 0))
    s0p = s0.reshape(b, pairs, LANES, hd)
    y, s_out = pl.pallas_call(
        functools.partial(_rwkv_kernel, L=L, Pb=pb, nC=t // L),
        grid=(b, npb),
        in_specs=[seq(r_off), seq(0), seq(0), seq(v_off), seq(0), seq(0), par, par, par, st_spec],
        out_specs=[seq(0), st_spec],
        out_shape=[jax.ShapeDtypeStruct((b, t, dm), F32), jax.ShapeDtypeStruct(s0p.shape, F32)],
        scratch_shapes=[pltpu.VMEM((pb, LANES, LANES), F32)],
        compiler_params=_cparams("parallel", "parallel"),
        name="rwkv_mix",
    )(three(r_src), three(ld), three(km), three(r_src), three(av), three(bv),
      row(r_k), row(ln_g), row(ln_b), s0p)
    return y.reshape(b * t, dm), s_out.reshape(s0.shape)


def _hgrn_lower_bounds(logits):
    p = jax.nn.softmax(logits.astype(F32), axis=0)
    return jnp.maximum(jnp.cumsum(p, axis=0) - p[0], 0.0)


def _mlstm_weights(w_in):
    dq = MLSTM_HEADS * MLSTM_DQK
    o = [0, dq, 2 * dq, 2 * dq + D_MIX, 2 * dq + D_MIX + MLSTM_HEADS, 2 * dq + D_MIX + 2 * MLSTM_HEADS]
    q, k, v = w_in[:, o[0]:o[1]], w_in[:, o[1]:o[2]], w_in[:, o[2]:o[3]]
    gates = w_in[:, o[3]:o[5]]
    rest = w_in[:, o[5]:]
    pad = lambda a: jnp.pad(a.reshape(-1, MLSTM_HEADS, MLSTM_DQK),
                            ((0, 0), (0, 0), (0, MLSTM_DQK_PAD - MLSTM_DQK))).reshape(-1, MLSTM_HEADS * MLSTM_DQK_PAD)
    main = jnp.concatenate([v, rest[:, :D_MIX], pad(q), pad(k), rest[:, D_MIX:]], axis=1).astype(BF16)
    gates = jnp.pad(gates, ((0, 0), (0, LANES - 2 * MLSTM_HEADS))).astype(BF16)
    return main, gates


def _rwkv_weights(w_in, mu):
    c = [0]
    for n in (D_MIX, RWKV_LORA, D_MIX, D_MIX, RWKV_LORA, D_GATE, D_XATTN):
        c.append(c[-1] + n)
    sl = lambda i: w_in[:, c[i]:c[i + 1]]
    main = jnp.concatenate([sl(0), sl(2), sl(3), sl(5), sl(6)], axis=1).astype(BF16)
    padl = lambda a: jnp.pad(a, ((0, 0), (0, RWKV_LORA_PAD - RWKV_LORA)))
    lora = jnp.concatenate([padl(sl(1)), padl(sl(4))], axis=1).astype(BF16)
    mu_ext = jnp.concatenate([mu.astype(F32), jnp.zeros((1, D_MODEL), F32)], axis=0)
    tile = 1024
    idx = [0] * (D_MIX // tile) + [2] * (D_MIX // tile) + [3] * (D_MIX // tile) + [5] * (D_GATE // tile) + [6] * (D_XATTN // tile)
    mu_main = mu_ext[jnp.array(idx)][:, None, :]
    mu_lora = mu_ext[jnp.array([1, 4])][:, None, :]
    return main, lora, mu_main, mu_lora


def _trunk(x3, mem_k, mem_v, s_hgrn, s_mc, s_mn, s_mm, s_rwkv, s_shift, prm, hgrn_hb, rwkv_pb):
    b, t, d = x3.shape
    x = x3.reshape(b * t, d)
    out = dict(hgrn=[], mc=[], mn=[], mm=[], rwkv=[], shift=[])
    for i in range(DEPTH):
        kind, j = i % N_MIXERS, i // N_MIXERS
        if kind == 0:
            h = rmsnorm(x, prm['norm_pre'][i], BF16)
            p = matmul(h, prm['hgrn_w'][j])
            mix, s = hgrn_mix(p, b, t, prm['hgrn_lb'][j], prm['hgrn_norm'][j], s_hgrn[j], hgrn_hb)
            out['hgrn'].append(s)
            gate_col, q_col = 3 * D_MIX, 3 * D_MIX + D_GATE
        elif kind == 1:
            h = rmsnorm(x, prm['norm_pre'][i], BF16)
            w_main, w_gates = prm['mlstm_w'][j]
            p = matmul(h, w_main)
            gates = matmul(h, w_gates, tn=LANES)[:, :2 * MLSTM_HEADS]
            mix, c, n, m = mlstm_mix(p, gates, b, t, prm['mlstm_b_gate'][j], prm['mlstm_norm'][j],
                                     s_mc[j], s_mn[j], s_mm[j])
            out['mc'].append(c)
            out['mn'].append(n)
            out['mm'].append(m)
            gate_col = 2 * D_MIX + 2 * MLSTM_HEADS * MLSTM_DQK_PAD
            q_col = gate_col + D_GATE
        else:
            h = rmsnorm(x, prm['norm_pre'][i], F32)
            h3 = h.reshape(b, t, d)
            hp = jnp.concatenate([s_shift[j][:, None, :], h3[:, :-1]], axis=1).reshape(b * t, d)
            w_main, w_lora, mu_main, mu_lora = prm['rwkv_w'][j]
            p = matmul_lerp(h, hp, mu_main, w_main)
            lora = matmul_lerp(h, hp, mu_lora, w_lora, tn=RWKV_LORA_PAD)
            ld, km, av, bv = rwkv_prep(p, D_MIX, lora, prm['rwkv_w2'][j], prm['rwkv_a2'][j], prm['rwkv_w0'][j],
                                       prm['rwkv_a0'][j], prm['rwkv_k_k'][j], prm['rwkv_k_a'][j])
            mix, s = rwkv_mix(p, 0, 2 * D_MIX, ld, km, av, bv, b, t, prm['rwkv_r_k'][j],
                              prm['rwkv_ln_g'][j], prm['rwkv_ln_b'][j], s_rwkv[j], rwkv_pb)
            out['rwkv'].append(s)
            out['shift'].append(h3[:, -1])
            gate_col, q_col = 3 * D_MIX, 3 * D_MIX + D_GATE
        xo = memattn(p, q_col, mem_k[i], mem_v[i], b, t)
        x = outproj(mix, xo, p, gate_col, prm['w_out'][i], x, prm['norm_post'][i])
    return (x.reshape(b, t, d), jnp.stack(out['hgrn']), jnp.stack(out['mc']), jnp.stack(out['mn']),
            jnp.stack(out['mm']), jnp.stack(out['rwkv']), jnp.stack(out['shift']))


def kernel(x_prompt, x_sample, mem_prompt, cache_mem_k, cache_mem_v, state_hgrn, state_mlstm_c, state_mlstm_n, state_mlstm_m, state_rwkv, state_rwkv_shift, norm_pre, norm_post, norm_mem, w_mem_kv, w_out, hgrn_w_in, hgrn_lb_logits, hgrn_norm, mlstm_w_in, mlstm_b_gate, mlstm_norm, rwkv_w_in, rwkv_mu, rwkv_w0, rwkv_w2, rwkv_a0, rwkv_a2, rwkv_k_k, rwkv_k_a, rwkv_r_k, rwkv_ln_g, rwkv_ln_b):
    n_h, n_m, n_r = hgrn_w_in.shape[0], mlstm_w_in.shape[0], rwkv_w_in.shape[0]
    pad_lora = lambda a: jnp.pad(a, ((0, RWKV_LORA_PAD - RWKV_LORA), (0, 0))).astype(BF16)
    prm = dict(
        norm_pre=norm_pre, norm_post=norm_post, w_out=w_out.astype(BF16),
        hgrn_w=[hgrn_w_in[j].astype(BF16) for j in range(n_h)],
        hgrn_lb=_hgrn_lower_bounds(hgrn_lb_logits), hgrn_norm=hgrn_norm,
        mlstm_w=[_mlstm_weights(mlstm_w_in[j]) for j in range(n_m)],
        mlstm_b_gate=mlstm_b_gate, mlstm_norm=mlstm_norm,
        rwkv_w=[_rwkv_weights(rwkv_w_in[j], rwkv_mu[j]) for j in range(n_r)],
        rwkv_w0=rwkv_w0, rwkv_w2=[pad_lora(rwkv_w2[j]) for j in range(n_r)],
        rwkv_a0=rwkv_a0, rwkv_a2=[pad_lora(rwkv_a2[j]) for j in range(n_r)],
        rwkv_k_k=rwkv_k_k, rwkv_k_a=rwkv_k_a, rwkv_r_k=rwkv_r_k, rwkv_ln_g=rwkv_ln_g, rwkv_ln_b=rwkv_ln_b)

    bp, n_mem, d = mem_prompt.shape
    mem2 = mem_prompt.reshape(bp * n_mem, d)
    mem_k, mem_v = [], []
    for i in range(DEPTH):
        kv = matmul(rmsnorm(mem2, norm_mem[i], BF16), w_mem_kv[i].astype(BF16))
        mem_k.append(kv[:, :D_XATTN].reshape(bp, n_mem, X_HEADS, X_HEAD_DIM))
        mem_v.append(kv[:, D_XATTN:].reshape(bp, n_mem, X_HEADS, X_HEAD_DIM))
    mem_k_prompt, mem_v_prompt = jnp.stack(mem_k), jnp.stack(mem_v)

    dt = x_prompt.dtype
    zeros = lambda *shape: jnp.zeros(shape, dt)
    (y_prompt, hgrn_p, mc_p, mn_p, mm_p, rwkv_p, shift_p) = _trunk(
        x_prompt, mem_k_prompt, mem_v_prompt,
        zeros(n_h, bp, HGRN_HEADS, HGRN_HEAD_DIM, HGRN_HEAD_DIM),
        zeros(n_m, bp, MLSTM_HEADS, MLSTM_DQK, MLSTM_DV),
        zeros(n_m, bp, MLSTM_HEADS, MLSTM_DQK),
        zeros(n_m, bp, MLSTM_HEADS),
        zeros(n_r, bp, RWKV_HEADS, RWKV_HEAD_DIM, RWKV_HEAD_DIM),
        zeros(n_r, bp, D_MODEL),
        prm, hgrn_hb=2, rwkv_pb=2)
    (y_sample, hgrn_s, mc_s, mn_s, mm_s, rwkv_s, shift_s) = _trunk(
        x_sample, cache_mem_k, cache_mem_v, state_hgrn, state_mlstm_c, state_mlstm_n, state_mlstm_m,
        state_rwkv, state_rwkv_shift, prm, hgrn_hb=4, rwkv_pb=4)
    return (y_prompt, y_sample, mem_k_prompt, mem_v_prompt,
            hgrn_p, mc_p, mn_p, mm_p, rwkv_p, shift_p,
            hgrn_s, mc_s, mn_s, mm_s, rwkv_s, shift_s)
```

```python
import functools
import math

import jax
import jax.numpy as jnp
from jax import lax
from jax.experimental import pallas as pl
from jax.experimental.pallas import tpu as pltpu

F32 = jnp.float32
BF16 = jnp.bfloat16

D_MODEL = 4096
DEPTH = 4
N_MIXERS = 3
D_MIX = 3 * D_MODEL // 4
D_XATTN = D_MODEL // 4
D_GATE = D_MIX + D_XATTN
N_MEM = 256
X_HEADS = 4
X_HEAD_DIM = D_XATTN // X_HEADS

HGRN_HEAD_DIM = 128
HGRN_HEADS = D_MIX // HGRN_HEAD_DIM
HGRN_CHUNK = 16

MLSTM_HEADS = 8
MLSTM_DV = D_MIX // MLSTM_HEADS
MLSTM_DQK = MLSTM_DV // 2
MLSTM_DQK_PAD = 256
MLSTM_CHUNK = 64

RWKV_HEAD_DIM = 64
RWKV_HEADS = D_MIX // RWKV_HEAD_DIM
RWKV_LORA = max(32, int(round(1.8 * math.sqrt(D_MIX) / 32)) * 32)
RWKV_LORA_PAD = 128
RWKV_GN_EPS = 64e-5
RWKV_CHUNK = 64
RWKV_ROWS = 256
RWKV_STREAMS = 3
RWKV_SEQ_BLOCK = 512

NORM_EPS = 1e-6
LOG2E = 1.4426950408889634
LANES = 128
SEQ_BLOCK = 512
VMEM_LIMIT = 48 * 1024 * 1024


def _cparams(*sem):
    return pltpu.CompilerParams(dimension_semantics=sem, vmem_limit_bytes=VMEM_LIMIT)


def _mm(a, b, **kw):
    return pl.dot(a.astype(BF16), b.astype(BF16), **kw)


def _sigmoid(x):
    return 1.0 / (1.0 + jnp.exp(-x))


def _log_sigmoid(x):
    return jnp.minimum(x, 0.0) - jnp.log1p(jnp.exp(-jnp.abs(x)))


def _cumsum_rows(x, block):
    row = lax.broadcasted_iota(jnp.int32, x.shape, 0) & (block - 1)
    sh = 1
    while sh < block:
        x = x + jnp.where(row >= sh, pltpu.roll(x, sh, axis=0), 0.0)
        sh *= 2
    return x


def _rmsnorm_kernel(x_ref, g_ref, o_ref):
    x = x_ref[...]
    ms = jnp.mean(x * x, axis=-1, keepdims=True)
    o_ref[...] = ((x * lax.rsqrt(ms + NORM_EPS)) * g_ref[...]).astype(o_ref.dtype)


def rmsnorm(x, gain, out_dtype, tm=256):
    m, d = x.shape
    tm = min(tm, m)
    return pl.pallas_call(
        _rmsnorm_kernel,
        grid=(m // tm,),
        in_specs=[pl.BlockSpec((tm, d), lambda i: (i, 0)), pl.BlockSpec((1, d), lambda i: (0, 0))],
        out_specs=pl.BlockSpec((tm, d), lambda i: (i, 0)),
        out_shape=jax.ShapeDtypeStruct((m, d), out_dtype),
        compiler_params=_cparams("parallel"),
        name="rmsnorm",
    )(x, gain.reshape(1, d))


def _mm_kernel(a_ref, w_ref, o_ref):
    o_ref[...] = _mm(a_ref[...], w_ref[...])


def matmul(a, w, tm=1024, tn=1024):
    m, k = a.shape
    n = w.shape[1]
    tm, tn = min(tm, m), min(tn, n)
    return pl.pallas_call(
        _mm_kernel,
        grid=(m // tm, n // tn),
        in_specs=[pl.BlockSpec((tm, k), lambda i, j: (i, 0)),
                  pl.BlockSpec((k, tn), lambda i, j: (0, j))],
        out_specs=pl.BlockSpec((tm, tn), lambda i, j: (i, j)),
        out_shape=jax.ShapeDtypeStruct((m, n), F32),
        compiler_params=_cparams("parallel", "parallel"),
        name="matmul",
    )(a, w)


def _mm_lerp_kernel(h_ref, hp_ref, mu_ref, w_ref, o_ref):
    @pl.when(pl.program_id(2) == 0)
    def _():
        o_ref[...] = jnp.zeros_like(o_ref)

    h = h_ref[...]
    lhs = h + (hp_ref[...] - h) * mu_ref[0]
    o_ref[...] += _mm(lhs, w_ref[...])


def matmul_lerp(h, hp, mu_tiles, w, tm=1024, tn=1024, tk=1024):
    m, k = h.shape
    n = w.shape[1]
    tm, tn, tk = min(tm, m), min(tn, n), min(tk, k)
    assert mu_tiles.shape == (n // tn, 1, k)
    return pl.pallas_call(
        _mm_lerp_kernel,
        grid=(m // tm, n // tn, k // tk),
        in_specs=[pl.BlockSpec((tm, tk), lambda i, j, kk: (i, kk)),
                  pl.BlockSpec((tm, tk), lambda i, j, kk: (i, kk)),
                  pl.BlockSpec((1, 1, tk), lambda i, j, kk: (j, 0, kk)),
                  pl.BlockSpec((tk, tn), lambda i, j, kk: (kk, j))],
        out_specs=pl.BlockSpec((tm, tn), lambda i, j, kk: (i, j)),
        out_shape=jax.ShapeDtypeStruct((m, n), F32),
        compiler_params=_cparams("parallel", "parallel", "arbitrary"),
        name="matmul_lerp",
    )(h, hp, mu_tiles, w)


def _outproj_kernel(mix_ref, xo_ref, g_ref, w_ref, x_ref, gain_ref, o_ref, acc_ref, *, k_mix):
    kk = pl.program_id(1)

    @pl.when(kk == 0)
    def _():
        acc_ref[...] = jnp.zeros_like(acc_ref)

    g = g_ref[...]
    u = jnp.where(kk < k_mix, mix_ref[...], xo_ref[...]) * (g * _sigmoid(g))
    acc_ref[...] += _mm(u, w_ref[...])

    @pl.when(kk == pl.num_programs(1) - 1)
    def _():
        y = acc_ref[...]
        ms = jnp.mean(y * y, axis=-1, keepdims=True)
        o_ref[...] = x_ref[...] + (y * lax.rsqrt(ms + NORM_EPS)) * gain_ref[...]


def outproj(mix, xo, p, gate_col, w_out, x, gain, tm=256, tk=1024):
    m, d_mix = mix.shape
    d_x = xo.shape[1]
    kdim, n = w_out.shape
    tm = min(tm, m)
    assert d_mix % tk == 0 and d_x % tk == 0 and gate_col % tk == 0 and kdim == d_mix + d_x
    k_mix, g_off = d_mix // tk, gate_col // tk
    return pl.pallas_call(
        functools.partial(_outproj_kernel, k_mix=k_mix),
        grid=(m // tm, kdim // tk),
        in_specs=[pl.BlockSpec((tm, tk), lambda i, kk: (i, jnp.minimum(kk, k_mix - 1))),
                  pl.BlockSpec((tm, tk), lambda i, kk: (i, jnp.maximum(kk - k_mix, 0))),
                  pl.BlockSpec((tm, tk), lambda i, kk: (i, g_off + kk)),
                  pl.BlockSpec((tk, n), lambda i, kk: (kk, 0)),
                  pl.BlockSpec((tm, n), lambda i, kk: (i, 0)),
                  pl.BlockSpec((1, n), lambda i, kk: (0, 0))],
        out_specs=pl.BlockSpec((tm, n), lambda i, kk: (i, 0)),
        out_shape=jax.ShapeDtypeStruct((m, n), F32),
        scratch_shapes=[pltpu.VMEM((tm, n), F32)],
        compiler_params=_cparams("parallel", "arbitrary"),
        name="outproj",
    )(mix, xo, p, w_out, x, gain.reshape(1, n))


def _memattn_kernel(q_ref, k_ref, v_ref, o_ref, *, heads, hd):
    scale = hd ** -0.5
    for h in range(heads):
        cs = slice(h * hd, (h + 1) * hd)
        s = _mm(q_ref[0, :, cs], k_ref[0, :, cs], trans_b=True) * scale
        s = s - jnp.max(s, axis=-1, keepdims=True)
        e = jnp.exp(s)
        p = e / jnp.sum(e, axis=-1, keepdims=True)
        o_ref[0, :, cs] = _mm(p, v_ref[0, :, cs])


def memattn(p, q_col, mk, mv, layer, k_col, v_col, b, t, heads, tq=512):
    n_mem = mk.shape[2]
    w = D_XATTN
    hd = w // heads
    tq = min(tq, t)
    assert q_col % w == 0 and k_col % w == 0 and v_col % w == 0
    q_off, k_off, v_off = q_col // w, k_col // w, v_col // w
    p3 = p.reshape(b, t, p.shape[1])
    out = pl.pallas_call(
        functools.partial(_memattn_kernel, heads=heads, hd=hd),
        grid=(b, t // tq),
        in_specs=[pl.BlockSpec((1, tq, w), lambda i, j: (i, j, q_off)),
                  pl.BlockSpec((None, 1, n_mem, w), lambda i, j: (layer, i, 0, k_off)),
                  pl.BlockSpec((None, 1, n_mem, w), lambda i, j: (layer, i, 0, v_off))],
        out_specs=pl.BlockSpec((1, tq, w), lambda i, j: (i, j, 0)),
        out_shape=jax.ShapeDtypeStruct((b, t, w), F32),
        compiler_params=_cparams("parallel", "parallel"),
        name="memattn",
    )(p3, mk, mv)
    return out.reshape(b * t, w)


def _hgrn_kernel(q_ref, f_ref, i_ref, loglb_ref, l1m_ref, oml_ref, g_ref, s0_ref,
                 o_ref, sout_ref, st_ref, *, L, Hb, nC):
    hd = HGRN_HEAD_DIM
    tstep = pl.program_id(2)

    @pl.when(tstep == 0)
    def _():
        for hh in range(Hb):
            st_ref[hh] = s0_ref[0, hh].T

    row = lax.broadcasted_iota(jnp.int32, (L, hd), 0)
    lane = lax.broadcasted_iota(jnp.int32, (L, hd), 1)

    def gates(c):
        r0 = pl.multiple_of(c * L, L)
        staged = []
        for hh in range(Hb):
            cs = slice(hh * hd, (hh + 1) * hd)
            q = q_ref[0, pl.ds(r0, L), cs]
            f = f_ref[0, pl.ds(r0, L), cs]
            vh = i_ref[0, pl.ds(r0, L), cs]
            qh = q * _sigmoid(q)
            a = loglb_ref[:, cs]
            cc = l1m_ref[:, cs] + _log_sigmoid(f)
            lf = jnp.maximum(a, cc) + jnp.log1p(jnp.exp(-jnp.abs(a - cc)))
            kh = oml_ref[:, cs] * _sigmoid(-f)
            b = _cumsum_rows(lf, L) * LOG2E
            attn = jnp.zeros((L, hd), F32)
            for s in range(L):
                dec = jnp.where(row >= s, jnp.exp2(b - b[s:s + 1, :]), 0.0)
                col = jnp.sum((qh * kh[s:s + 1, :]) * dec, axis=-1, keepdims=True)
                attn = jnp.where(lane == s, col, attn)
            b_last = b[L - 1:L, :]
            staged.append(((qh * jnp.exp2(b)).astype(BF16), attn[:, :L].astype(BF16), vh.astype(BF16),
                           (kh * jnp.exp2(b_last - b)).astype(BF16), jnp.exp2(b_last)))
        return tuple(staged)

    def update(c, staged):
        r0 = pl.multiple_of(c * L, L)
        for hh in range(Hb):
            cs = slice(hh * hd, (hh + 1) * hd)
            qe, attn, vh, kd, e_last = staged[hh]
            st = st_ref[hh]
            o = pl.dot(qe, st.astype(BF16), trans_b=True) + pl.dot(attn, vh)
            o = o * lax.rsqrt(jnp.mean(o * o, axis=-1, keepdims=True) + NORM_EPS)
            o_ref[0, pl.ds(r0, L), cs] = o * g_ref[:, cs]
            st_ref[hh] = st * e_last + pl.dot(vh, kd, trans_a=True)

    def body(c, staged):
        nxt = gates(c + 1)
        update(c, staged)
        return nxt

    update(nC - 1, lax.fori_loop(0, nC - 1, body, gates(0)))

    @pl.when(tstep == pl.num_programs(2) - 1)
    def _():
        for hh in range(Hb):
            sout_ref[0, hh] = st_ref[hh].T


def hgrn_mix(p, b, t, lb, norm_g, s_all, layer, hb):
    heads, hd = s_all.shape[2], s_all.shape[3]
    dm = heads * hd
    L = math.gcd(t, HGRN_CHUNK)
    tb = min(t, SEQ_BLOCK)
    w = hb * hd
    nhb = heads // hb
    p3 = p.reshape(b, t, p.shape[1])
    row = lambda a: a.reshape(1, dm).astype(F32)
    par_spec = pl.BlockSpec((1, w), lambda i, j, tt: (0, j))
    mix, s_out = pl.pallas_call(
        functools.partial(_hgrn_kernel, L=L, Hb=hb, nC=tb // L),
        grid=(b, nhb, t // tb),
        in_specs=[pl.BlockSpec((1, tb, w), lambda i, j, tt: (i, tt, j)),
                  pl.BlockSpec((1, tb, w), lambda i, j, tt: (i, tt, nhb + j)),
                  pl.BlockSpec((1, tb, w), lambda i, j, tt: (i, tt, 2 * nhb + j)),
                  par_spec, par_spec, par_spec, par_spec,
                  pl.BlockSpec((None, 1, hb, hd, hd), lambda i, j, tt: (layer, i, j, 0, 0))],
        out_specs=[pl.BlockSpec((1, tb, w), lambda i, j, tt: (i, tt, j)),
                   pl.BlockSpec((1, hb, hd, hd), lambda i, j, tt: (i, j, 0, 0))],
        out_shape=[jax.ShapeDtypeStruct((b, t, dm), F32), jax.ShapeDtypeStruct(s_all.shape[1:], F32)],
        scratch_shapes=[pltpu.VMEM((hb, hd, hd), F32)],
        compiler_params=_cparams("parallel", "parallel", "arbitrary"),
        name="hgrn_mix",
    )(p3, p3, p3, row(jnp.log(lb)), row(jnp.log1p(-lb)), row(1.0 - lb), row(norm_g), s_all)
    return mix.reshape(b * t, dm), s_out


def _mlstm_kernel(bg_ref, q_ref, k_ref, v_ref, op_ref, ig_ref, fg_ref, g_ref, c0_ref, n0_ref, m0_ref,
                  o_ref, cout_ref, nout_ref, mout_ref, c_sc, n_sc, m_sc, *, L, nC, heads, dqk):
    h = pl.program_id(1)
    b_i = bg_ref[h]
    b_f = bg_ref[heads + h]
    c_sc[...] = jnp.zeros_like(c_sc)
    c_sc[0:dqk, :] = c0_ref[0, 0]
    n_sc[...] = jnp.zeros_like(n_sc)
    n_sc[:, 0:dqk] = n0_ref[0, 0]
    m_sc[...] = m0_ref[0, 0]
    ti = lax.broadcasted_iota(jnp.int32, (L, L), 0)
    si = lax.broadcasted_iota(jnp.int32, (L, L), 1)
    eye, low = ti == si, si <= ti
    col = lambda r: jnp.sum(jnp.where(eye, r, 0.0), axis=1, keepdims=True)
    kscale = dqk ** -0.5

    def chunk(c, carry):
        r0 = pl.multiple_of(c * L, L)
        q = q_ref[0, pl.ds(r0, L), :]
        k = k_ref[0, pl.ds(r0, L), :] * kscale
        v = v_ref[0, pl.ds(r0, L), :]
        ic_row = ig_ref[0, 0, pl.ds(c, 1), :] + b_i
        lf_row = _log_sigmoid(fg_ref[0, 0, pl.ds(c, 1), :] + b_f)
        b_col = jnp.sum(jnp.where(low, lf_row, 0.0), axis=1, keepdims=True)
        b_row = jnp.sum(jnp.where(ti <= si, col(lf_row), 0.0), axis=0, keepdims=True)
        log_w = jnp.where(low, b_col - b_row + ic_row, -jnp.inf)
        m = m_sc[...]
        log_inter = b_col + m
        m_t = jnp.maximum(log_inter, jnp.max(log_w, axis=1, keepdims=True))
        w = jnp.exp(log_w - m_t)
        w_inter = jnp.exp(log_inter - m_t)
        a = w * _mm(q, k, trans_b=True)
        cst, n = c_sc[...], n_sc[...]
        num = w_inter * _mm(q, cst) + _mm(a, v)
        den = w_inter * jnp.sum(q * n, axis=1, keepdims=True) + jnp.sum(a, axis=1, keepdims=True)
        hh = num / jnp.maximum(jnp.abs(den), jnp.exp(-m_t))
        m_last = m_t[L - 1:L, :]
        b_last = b_col[L - 1:L, :]
        w_state = jnp.exp(b_last - b_col + col(ic_row) - m_last)
        decay = jnp.exp(b_last + m - m_last)
        kw = w_state * k
        c_sc[...] = decay * cst + _mm(kw, v, trans_a=True)
        n_sc[...] = decay * n + jnp.sum(kw, axis=0, keepdims=True)
        m_sc[...] = m_last
        hn = hh * lax.rsqrt(jnp.mean(hh * hh, axis=1, keepdims=True) + NORM_EPS) * g_ref[...]
        o_ref[0, pl.ds(r0, L), :] = _sigmoid(op_ref[0, pl.ds(r0, L), :]) * hn
        return carry

    lax.fori_loop(0, nC, chunk, 0)
    cout_ref[0, 0] = c_sc[0:dqk, :]
    nout_ref[0, 0] = n_sc[:, 0:dqk]
    mout_ref[0, 0] = m_sc[...]


def mlstm_mix(p, gates, b, t, b_gate, norm_g, c0, n0, m0):
    heads, dqk, dv = c0.shape[1], c0.shape[2], c0.shape[3]
    dqp = MLSTM_DQK_PAD
    dm = heads * dv
    L = math.gcd(t, MLSTM_CHUNK)
    nC = t // L
    p3 = p.reshape(b, t, p.shape[1])
    g3 = gates.reshape(b, t, 2 * heads)
    ig = jnp.swapaxes(g3[..., :heads], 1, 2).reshape(b, heads, nC, L)
    fg = jnp.swapaxes(g3[..., heads:], 1, 2).reshape(b, heads, nC, L)
    q_off, k_off = 2 * dm // dqp, 2 * dm // dqp + heads
    gate_spec = pl.BlockSpec((1, 1, nC, L), lambda i, j: (i, j, 0, 0))
    c_spec = pl.BlockSpec((1, 1, dqk, dv), lambda i, j: (i, j, 0, 0))
    n_spec = pl.BlockSpec((1, 1, 1, dqk), lambda i, j: (i, j, 0, 0))
    m_spec = pl.BlockSpec((1, 1, 1, 1), lambda i, j: (i, j, 0, 0))
    mix, c, n, m = pl.pallas_call(
        functools.partial(_mlstm_kernel, L=L, nC=nC, heads=heads, dqk=dqk),
        grid=(b, heads),
        in_specs=[pl.BlockSpec(memory_space=pltpu.SMEM),
                  pl.BlockSpec((1, t, dqp), lambda i, j: (i, 0, q_off + j)),
                  pl.BlockSpec((1, t, dqp), lambda i, j: (i, 0, k_off + j)),
                  pl.BlockSpec((1, t, dv), lambda i, j: (i, 0, j)),
                  pl.BlockSpec((1, t, dv), lambda i, j: (i, 0, heads + j)),
                  gate_spec, gate_spec,
                  pl.BlockSpec((1, dv), lambda i, j: (0, j)),
                  c_spec, n_spec, m_spec],
        out_specs=[pl.BlockSpec((1, t, dv), lambda i, j: (i, 0, j)), c_spec, n_spec, m_spec],
        out_shape=[jax.ShapeDtypeStruct((b, t, dm), F32),
                   jax.ShapeDtypeStruct((b, heads, dqk, dv), F32),
                   jax.ShapeDtypeStruct((b, heads, 1, dqk), F32),
                   jax.ShapeDtypeStruct((b, heads, 1, 1), F32)],
        scratch_shapes=[pltpu.VMEM((dqp, dv), F32), pltpu.VMEM((1, dqp), F32), pltpu.VMEM((1, 1), F32)],
        compiler_params=_cparams("parallel", "parallel"),
        name="mlstm_mix",
    )(b_gate.astype(F32), p3, p3, p3, p3, ig, fg, norm_g.reshape(1, dm).astype(F32),
      c0, n0.reshape(b, heads, 1, dqk), m0.reshape(b, heads, 1, 1))
    return mix.reshape(b * t, dm), c, n.reshape(b, heads, dqk), m.reshape(b, heads)


def _seg_sum(x, lane_lo):
    s0 = jnp.sum(jnp.where(lane_lo, x, 0.0), axis=-1, keepdims=True)
    s1 = jnp.sum(jnp.where(lane_lo, 0.0, x), axis=-1, keepdims=True)
    return jnp.where(lane_lo, s0, s1)


def _rwkv_prep_kernel(k_ref, wlo_ref, alo_ref, w2_ref, a2_ref, w0_ref, a0_ref, kk_ref, ka_ref,
                      ld_ref, lc_ref, km_ref, av_ref, bv_ref, *, L):
    k = k_ref[...]
    lw = w0_ref[...] + _mm(jnp.tanh(wlo_ref[...]), w2_ref[...])
    log_w = -(jnp.maximum(-lw, 0.0) + jnp.log1p(jnp.exp(-jnp.abs(lw)))) - 0.5
    ld = -jnp.exp(log_w)
    ld_ref[...] = ld
    lc_ref[...] = _cumsum_rows(ld, L)
    a = _sigmoid(a0_ref[...] + _mm(alo_ref[...], a2_ref[...]))
    kk = k * kk_ref[...]
    tn = k.shape[1]
    lane_lo = lax.broadcasted_iota(jnp.int32, (k.shape[0], LANES), 1) < RWKV_HEAD_DIM
    for c in range(tn // LANES):
        cs = slice(c * LANES, (c + 1) * LANES)
        kc = kk[:, cs]
        kc = kc / jnp.maximum(jnp.sqrt(_seg_sum(kc * kc, lane_lo)), 1e-12)
        av_ref[:, cs] = -kc
        bv_ref[:, cs] = kc * a[:, cs]
    km_ref[...] = k * (1.0 + (a - 1.0) * ka_ref[...])


def rwkv_prep(p, k_col, lora, w2, a2, w0, a0, k_k, k_a, L, tm=512, tn=512):
    m = p.shape[0]
    dm = w2.shape[1]
    lp = RWKV_LORA_PAD
    tm = min(tm, m)
    assert tm % L == 0
    k_off = k_col // tn
    par = pl.BlockSpec((1, tn), lambda i, j: (0, j))
    out = pl.BlockSpec((tm, tn), lambda i, j: (i, j))
    row = lambda a: a.reshape(1, dm).astype(F32)
    return pl.pallas_call(
        functools.partial(_rwkv_prep_kernel, L=L),
        grid=(m // tm, dm // tn),
        in_specs=[pl.BlockSpec((tm, tn), lambda i, j: (i, k_off + j)),
                  pl.BlockSpec((tm, lp), lambda i, j: (i, 0)),
                  pl.BlockSpec((tm, lp), lambda i, j: (i, 1)),
                  pl.BlockSpec((lp, tn), lambda i, j: (0, j)),
                  pl.BlockSpec((lp, tn), lambda i, j: (0, j)),
                  par, par, par, par],
        out_specs=[out] * 5,
        out_shape=[jax.ShapeDtypeStruct((m, dm), F32)] * 5,
        compiler_params=_cparams("parallel", "parallel"),
        name="rwkv_prep",
    )(p, lora, lora, w2, a2, row(w0), row(a0), row(k_k), row(k_a))


def _rwkv_kernel(r_ref, ld_ref, lc_ref, k_ref, v_ref, a_ref, b_ref, rk_ref, lg_ref, lb_ref, s0_ref,
                 y_ref, sout_ref, sp_ref, *, L, Gp, Ns, nC):
    hd = RWKV_HEAD_DIM
    G = 2 * Gp
    R = G * L
    W = Gp * LANES
    tstep = pl.program_id(2)
    bm = (lax.broadcasted_iota(jnp.int32, (LANES, LANES), 0) // hd
          == lax.broadcasted_iota(jnp.int32, (LANES, LANES), 1) // hd)

    @pl.when(tstep == 0)
    def _():
        for p in range(Ns * Gp):
            x = s0_ref[0, p]
            sp_ref[p] = jnp.where(bm, jnp.concatenate([x, x], axis=1), 0.0)

    head_mask = (lax.broadcasted_iota(jnp.int32, (R, W), 0) // L
                 == lax.broadcasted_iota(jnp.int32, (R, W), 1) // hd)
    ri = lax.broadcasted_iota(jnp.int32, (R, R), 0)
    ci = lax.broadcasted_iota(jnp.int32, (R, R), 1)
    same = (ri // L) == (ci // L)
    strict = same & (ci < ri)
    ri2 = lax.broadcasted_iota(jnp.int32, (R, 2 * R), 0)
    ci2 = lax.broadcasted_iota(jnp.int32, (R, 2 * R), 1)
    ci2 = jnp.where(ci2 >= R, ci2 - R, ci2)
    incl2 = ((ri2 // L) == (ci2 // L)) & (ci2 <= ri2)
    lo = (lax.broadcasted_iota(jnp.int32, (L, LANES), 1) < hd)
    n_dbl = max(1, (L - 1).bit_length())
    stack = lambda x: jnp.where(head_mask, jnp.concatenate([x] * G, axis=0), 0.0).astype(BF16)

    streams = range(Ns)

    def chunk(c, carry):
        r0 = pl.multiple_of(c * L, L)
        ld_ = lambda ref, s: ref[0, pl.ds(r0, L), s * W:(s + 1) * W]
        ar2, bk2, v2, bkc, c_last = [], [], [], [], []
        for s in streams:
            r, ldec, lc, k, v, av, bv = (ld_(r_ref, s), ld_(ld_ref, s), ld_(lc_ref, s), ld_(k_ref, s),
                                         ld_(v_ref, s), ld_(a_ref, s), ld_(b_ref, s))
            lcl = lc[L - 1:L, :]
            e_neg = jnp.exp(-lc)
            e_rem = jnp.exp(lcl - lc)
            ar2.append(jnp.concatenate([stack(av * jnp.exp(lc - ldec)), stack(r * jnp.exp(lc))], axis=0))
            bk2.append(jnp.concatenate([stack(bv * e_neg), stack(k * e_neg)], axis=0))
            v2.append(stack(v))
            bkc.append(jnp.concatenate([stack(bv * e_rem), stack(k * e_rem)], axis=0))
            c_last.append(jnp.exp(lcl))
        g = [pl.dot(ar2[s], bk2[s], trans_b=True) for s in streams]
        nmat = [jnp.where(strict, g[s][:R, :R], 0.0).astype(BF16) for s in streams]
        a_ak = [jnp.where(strict, g[s][:R, R:], 0.0).astype(BF16) for s in streams]
        a_rbk = [jnp.where(incl2, g[s][R:, :], 0.0).astype(BF16) for s in streams]
        akv = [pl.dot(a_ak[s], v2[s]) for s in streams]
        minv = [nmat[s].astype(F32) for s in streams]
        for it in range(1, n_dbl):
            nmat = [pl.dot(nmat[s], nmat[s]).astype(BF16) for s in streams]
            minv = [minv[s] + nmat[s] + pl.dot(nmat[s], minv[s].astype(BF16)) for s in streams]
        ur0 = [jnp.concatenate(
            [pl.dot(ar2[s][:, p * LANES:(p + 1) * LANES], sp_ref[s * Gp + p].astype(BF16), trans_b=True)
             for p in range(Gp)], axis=1) for s in streams]
        x = [ur0[s][:R] + akv[s] for s in streams]
        x = [x[s] + pl.dot(minv[s].astype(BF16), x[s].astype(BF16)) for s in streams]
        xv = [jnp.concatenate([x[s].astype(BF16), v2[s]], axis=0) for s in streams]
        y2 = [ur0[s][R:] + pl.dot(a_rbk[s], xv[s]) for s in streams]
        for s in streams:
            y = y2[s][0:L]
            for j in range(1, G):
                y = y + y2[s][j * L:(j + 1) * L]
            for p in range(Gp):
                cs = slice(p * LANES, (p + 1) * LANES)
                gs = slice(s * W + p * LANES, s * W + (p + 1) * LANES)
                q = s * Gp + p
                sp_ref[q] = sp_ref[q] * c_last[s][:, cs] + pl.dot(xv[s][:, cs], bkc[s][:, cs], trans_a=True)
                yp = y[:, cs]
                rp, kp, vp = r_ref[0, pl.ds(r0, L), gs], k_ref[0, pl.ds(r0, L), gs], v_ref[0, pl.ds(r0, L), gs]
                mean = _seg_sum(yp, lo) * (1.0 / hd)
                yc = yp - mean
                var = _seg_sum(yc * yc, lo) * (1.0 / hd)
                yn = yc * lax.rsqrt(var + RWKV_GN_EPS) * lg_ref[:, gs] + lb_ref[:, gs]
                bonus = _seg_sum(rp * kp * rk_ref[:, gs], lo) * vp
                y_ref[0, pl.ds(r0, L), gs] = yn + bonus
        return carry

    lax.fori_loop(0, nC, chunk, 0)

    @pl.when(tstep == pl.num_programs(2) - 1)
    def _():
        row_lo = lax.broadcasted_iota(jnp.int32, (LANES, hd), 0) < hd
        for p in range(Ns * Gp):
            sp = sp_ref[p]
            sout_ref[0, p] = jnp.where(row_lo, sp[:, :hd], sp[:, hd:])


def rwkv_mix(r_src, r_col, v_col, ld, lc, km, av, bv, b, t, L, r_k, ln_g, ln_b, s0):
    heads, hd = s0.shape[1], s0.shape[2]
    dm = heads * hd
    pairs = heads // 2
    gp = max(g for g in range(1, pairs + 1)
             if pairs % g == 0 and 2 * g * L <= RWKV_ROWS and (2 * g * L) % LANES == 0)
    ns = max(n for n in range(1, RWKV_STREAMS + 1) if (pairs // gp) % n == 0)
    tb = min(t, RWKV_SEQ_BLOCK)
    w = ns * gp * LANES
    r_off, v_off = r_col // w, v_col // w
    three = lambda a: a.reshape(b, t, a.shape[1])
    row = lambda a: a.reshape(1, dm).astype(F32)
    seq = lambda off: pl.BlockSpec((1, tb, w), lambda i, j, tt: (i, tt, off + j))
    par = pl.BlockSpec((1, w), lambda i, j, tt: (0, j))
    st_spec = pl.BlockSpec((1, ns * gp, LANES, hd), lambda i, j, tt: (i, j, 0, 0))
    s0p = s0.reshape(b, pairs, LANES, hd)
    y, s_out = pl.pallas_call(
        functools.partial(_rwkv_kernel, L=L, Gp=gp, Ns=ns, nC=tb // L),
        grid=(b, pairs // (gp * ns), t // tb),
        in_specs=[seq(r_off), seq(0), seq(0), seq(0), seq(v_off), seq(0), seq(0), par, par, par, st_spec],
        out_specs=[seq(0), st_spec],
        out_shape=[jax.ShapeDtypeStruct((b, t, dm), F32), jax.ShapeDtypeStruct(s0p.shape, F32)],
        scratch_shapes=[pltpu.VMEM((ns * gp, LANES, LANES), F32)],
        compiler_params=_cparams("parallel", "parallel", "arbitrary"),
        name="rwkv_mix",
    )(three(r_src), three(ld), three(lc), three(km), three(r_src), three(av), three(bv),
      row(r_k), row(ln_g), row(ln_b), s0p)
    return y.reshape(b * t, dm), s_out.reshape(s0.shape)


def _hgrn_lower_bounds(logits):
    p = jax.nn.softmax(logits.astype(F32), axis=0)
    return jnp.maximum(jnp.cumsum(p, axis=0) - p[0], 0.0)


def _mlstm_weights(w_in):
    dq = MLSTM_HEADS * MLSTM_DQK
    o = [0, dq, 2 * dq, 2 * dq + D_MIX, 2 * dq + D_MIX + MLSTM_HEADS, 2 * dq + D_MIX + 2 * MLSTM_HEADS]
    q, k, v = w_in[:, o[0]:o[1]], w_in[:, o[1]:o[2]], w_in[:, o[2]:o[3]]
    gates = w_in[:, o[3]:o[5]]
    rest = w_in[:, o[5]:]
    pad = lambda a: jnp.pad(a.reshape(-1, MLSTM_HEADS, MLSTM_DQK),
                            ((0, 0), (0, 0), (0, MLSTM_DQK_PAD - MLSTM_DQK))).reshape(-1, MLSTM_HEADS * MLSTM_DQK_PAD)
    main = jnp.concatenate([v, rest[:, :D_MIX], pad(q), pad(k), rest[:, D_MIX:]], axis=1).astype(BF16)
    gates = jnp.pad(gates, ((0, 0), (0, LANES - 2 * MLSTM_HEADS))).astype(BF16)
    return main, gates


def _rwkv_weights(w_in, mu):
    c = [0]
    for n in (D_MIX, RWKV_LORA, D_MIX, D_MIX, RWKV_LORA, D_GATE, D_XATTN):
        c.append(c[-1] + n)
    sl = lambda i: w_in[:, c[i]:c[i + 1]]
    main = jnp.concatenate([sl(0), sl(2), sl(3), sl(5), sl(6)], axis=1).astype(BF16)
    padl = lambda a: jnp.pad(a, ((0, 0), (0, RWKV_LORA_PAD - RWKV_LORA)))
    lora = jnp.concatenate([padl(sl(1)), padl(sl(4))], axis=1).astype(BF16)
    mu_ext = jnp.concatenate([mu.astype(F32), jnp.zeros((1, D_MODEL), F32)], axis=0)
    tile = 1024
    idx = [0] * (D_MIX // tile) + [2] * (D_MIX // tile) + [3] * (D_MIX // tile) + [5] * (D_GATE // tile) + [6] * (D_XATTN // tile)
    mu_main = mu_ext[jnp.array(idx)][:, None, :]
    mu_lora = mu_ext[jnp.array([1, 4])][:, None, :]
    return main, lora, mu_main, mu_lora


def _trunk(x3, mk, mv, mem_cols, s_hgrn, s_mc, s_mn, s_mm, s_rwkv, s_shift, prm, hgrn_hb):
    b, t, d = x3.shape
    x = x3.reshape(b * t, d)
    out = dict(hgrn=[], mc=[], mn=[], mm=[], rwkv=[], shift=[])
    for i in range(DEPTH):
        kind, j = i % N_MIXERS, i // N_MIXERS
        if kind == 0:
            h = rmsnorm(x, prm['norm_pre'][i], BF16)
            p = matmul(h, prm['hgrn_w'][j])
            mix, s = hgrn_mix(p, b, t, prm['hgrn_lb'][j], prm['hgrn_norm'][j], s_hgrn, j, hgrn_hb)
            out['hgrn'].append(s)
            gate_col, q_col = 3 * D_MIX, 3 * D_MIX + D_GATE
        elif kind == 1:
            h = rmsnorm(x, prm['norm_pre'][i], BF16)
            w_main, w_gates = prm['mlstm_w'][j]
            p = matmul(h, w_main)
            gates = matmul(h, w_gates, tn=LANES)[:, :2 * MLSTM_HEADS]
            mix, c, n, m = mlstm_mix(p, gates, b, t, prm['mlstm_b_gate'][j], prm['mlstm_norm'][j],
                                     s_mc[j], s_mn[j], s_mm[j])
            out['mc'].append(c)
            out['mn'].append(n)
            out['mm'].append(m)
            gate_col = 2 * D_MIX + 2 * MLSTM_HEADS * MLSTM_DQK_PAD
            q_col = gate_col + D_GATE
        else:
            h = rmsnorm(x, prm['norm_pre'][i], F32)
            h3 = h.reshape(b, t, d)
            hp = jnp.concatenate([s_shift[j][:, None, :], h3[:, :-1]], axis=1).reshape(b * t, d)
            w_main, w_lora, mu_main, mu_lora = prm['rwkv_w'][j]
            p = matmul_lerp(h, hp, mu_main, w_main)
            lora = matmul_lerp(h, hp, mu_lora, w_lora, tn=RWKV_LORA_PAD)
            L = math.gcd(t, RWKV_CHUNK)
            ld, lc, km, av, bv = rwkv_prep(p, D_MIX, lora, prm['rwkv_w2'][j], prm['rwkv_a2'][j], prm['rwkv_w0'][j],
                                           prm['rwkv_a0'][j], prm['rwkv_k_k'][j], prm['rwkv_k_a'][j], L)
            mix, s = rwkv_mix(p, 0, 2 * D_MIX, ld, lc, km, av, bv, b, t, L, prm['rwkv_r_k'][j],
                              prm['rwkv_ln_g'][j], prm['rwkv_ln_b'][j], s_rwkv[j])
            out['rwkv'].append(s)
            out['shift'].append(h3[:, -1])
            gate_col, q_col = 3 * D_MIX, 3 * D_MIX + D_GATE
        lyr, k_col, v_col = mem_cols(i)
        xo = memattn(p, q_col, mk[i] if isinstance(mk, list) else mk, mv[i] if isinstance(mv, list) else mv,
                     lyr, k_col, v_col, b, t, X_HEADS)
        x = outproj(mix, xo, p, gate_col, prm['w_out'][i], x, prm['norm_post'][i])
    return (x.reshape(b, t, d), jnp.stack(out['hgrn']), jnp.stack(out['mc']), jnp.stack(out['mn']),
            jnp.stack(out['mm']), jnp.stack(out['rwkv']), jnp.stack(out['shift']))


def kernel(x_prompt, x_sample, mem_prompt, cache_mem_k, cache_mem_v, state_hgrn, state_mlstm_c, state_mlstm_n, state_mlstm_m, state_rwkv, state_rwkv_shift, norm_pre, norm_post, norm_mem, w_mem_kv, w_out, hgrn_w_in, hgrn_lb_logits, hgrn_norm, mlstm_w_in, mlstm_b_gate, mlstm_norm, rwkv_w_in, rwkv_mu, rwkv_w0, rwkv_w2, rwkv_a0, rwkv_a2, rwkv_k_k, rwkv_k_a, rwkv_r_k, rwkv_ln_g, rwkv_ln_b):
    n_h, n_m, n_r = hgrn_w_in.shape[0], mlstm_w_in.shape[0], rwkv_w_in.shape[0]
    pad_lora = lambda a: jnp.pad(a, ((0, RWKV_LORA_PAD - RWKV_LORA), (0, 0))).astype(BF16)
    prm = dict(
        norm_pre=norm_pre, norm_post=norm_post, w_out=w_out.astype(BF16),
        hgrn_w=[hgrn_w_in[j].astype(BF16) for j in range(n_h)],
        hgrn_lb=_hgrn_lower_bounds(hgrn_lb_logits), hgrn_norm=hgrn_norm,
        mlstm_w=[_mlstm_weights(mlstm_w_in[j]) for j in range(n_m)],
        mlstm_b_gate=mlstm_b_gate, mlstm_norm=mlstm_norm,
        rwkv_w=[_rwkv_weights(rwkv_w_in[j], rwkv_mu[j]) for j in range(n_r)],
        rwkv_w0=rwkv_w0, rwkv_w2=[pad_lora(rwkv_w2[j]) for j in range(n_r)],
        rwkv_a0=rwkv_a0, rwkv_a2=[pad_lora(rwkv_a2[j]) for j in range(n_r)],
        rwkv_k_k=rwkv_k_k, rwkv_k_a=rwkv_k_a, rwkv_r_k=rwkv_r_k, rwkv_ln_g=rwkv_ln_g, rwkv_ln_b=rwkv_ln_b)

    bp, n_mem, d = mem_prompt.shape
    mem2 = mem_prompt.reshape(bp * n_mem, d)
    kvs = [matmul(rmsnorm(mem2, norm_mem[i], BF16), w_mem_kv[i].astype(BF16)).reshape(1, bp, n_mem, 2 * D_XATTN)
           for i in range(DEPTH)]
    heads_shape = (bp, n_mem, X_HEADS, X_HEAD_DIM)
    mem_k_prompt = jnp.stack([kv[0, :, :, :D_XATTN].reshape(heads_shape) for kv in kvs])
    mem_v_prompt = jnp.stack([kv[0, :, :, D_XATTN:].reshape(heads_shape) for kv in kvs])

    dt = x_prompt.dtype
    zeros = lambda *shape: jnp.zeros(shape, dt)
    (y_prompt, hgrn_p, mc_p, mn_p, mm_p, rwkv_p, shift_p) = _trunk(
        x_prompt, kvs, kvs, lambda i: (0, 0, D_XATTN),
        zeros(n_h, bp, HGRN_HEADS, HGRN_HEAD_DIM, HGRN_HEAD_DIM),
        zeros(n_m, bp, MLSTM_HEADS, MLSTM_DQK, MLSTM_DV),
        zeros(n_m, bp, MLSTM_HEADS, MLSTM_DQK),
        zeros(n_m, bp, MLSTM_HEADS),
        zeros(n_r, bp, RWKV_HEADS, RWKV_HEAD_DIM, RWKV_HEAD_DIM),
        zeros(n_r, bp, D_MODEL),
        prm, hgrn_hb=6)
    bs = x_sample.shape[0]
    ck = cache_mem_k.reshape(DEPTH, bs, n_mem, D_XATTN)
    cv = cache_mem_v.reshape(DEPTH, bs, n_mem, D_XATTN)
    (y_sample, hgrn_s, mc_s, mn_s, mm_s, rwkv_s, shift_s) = _trunk(
        x_sample, ck, cv, lambda i: (i, 0, 0), state_hgrn, state_mlstm_c, state_mlstm_n, state_mlstm_m,
        state_rwkv, state_rwkv_shift, prm, hgrn_hb=8)
    return (y_prompt, y_sample, mem_k_prompt, mem_v_prompt,
            hgrn_p, mc_p, mn_p, mm_p, rwkv_p, shift_p,
            hgrn_s, mc_s, mn_s, mm_s, rwkv_s, shift_s)
```

```python
import functools
import math

import jax
import jax.numpy as jnp
from jax import lax
from jax.experimental import pallas as pl
from jax.experimental.pallas import tpu as pltpu

F32 = jnp.float32
BF16 = jnp.bfloat16

D_MODEL = 4096
DEPTH = 4
N_MIXERS = 3
D_MIX = 3 * D_MODEL // 4
D_XATTN = D_MODEL // 4
D_GATE = D_MIX + D_XATTN
N_MEM = 256
X_HEADS = 4
X_HEAD_DIM = D_XATTN // X_HEADS

HGRN_HEAD_DIM = 128
HGRN_HEADS = D_MIX // HGRN_HEAD_DIM
HGRN_CHUNK = 16

MLSTM_HEADS = 8
MLSTM_DV = D_MIX // MLSTM_HEADS
MLSTM_DQK = MLSTM_DV // 2
MLSTM_DQK_PAD = 256
MLSTM_CHUNK = 64

RWKV_HEAD_DIM = 64
RWKV_HEADS = D_MIX // RWKV_HEAD_DIM
RWKV_LORA = max(32, int(round(1.8 * math.sqrt(D_MIX) / 32)) * 32)
RWKV_LORA_PAD = 128
RWKV_GN_EPS = 64e-5
RWKV_CHUNK = 64
RWKV_ROWS = 256
RWKV_STREAMS = 3
RWKV_SEQ_BLOCK = 512

NORM_EPS = 1e-6
LOG2E = 1.4426950408889634
LANES = 128
SEQ_BLOCK = 512
VMEM_LIMIT = 48 * 1024 * 1024
VMEM_LIMIT_OUTPROJ = 58 * 1024 * 1024


def _cparams(*sem, vmem=VMEM_LIMIT):
    return pltpu.CompilerParams(dimension_semantics=sem, vmem_limit_bytes=vmem)


def _mm(a, b, **kw):
    return pl.dot(a.astype(BF16), b.astype(BF16), **kw)


def _sigmoid(x):
    return 1.0 / (1.0 + jnp.exp(-x))


def _log_sigmoid(x):
    return jnp.minimum(x, 0.0) - jnp.log1p(jnp.exp(-jnp.abs(x)))


def _cumsum_rows(x, block):
    row = lax.broadcasted_iota(jnp.int32, x.shape, 0) & (block - 1)
    sh = 1
    while sh < block:
        x = x + jnp.where(row >= sh, pltpu.roll(x, sh, axis=0), 0.0)
        sh *= 2
    return x


def _rmsnorm_kernel(x_ref, g_ref, o_ref):
    x = x_ref[...]
    ms = jnp.mean(x * x, axis=-1, keepdims=True)
    o_ref[...] = ((x * lax.rsqrt(ms + NORM_EPS)) * g_ref[...]).astype(o_ref.dtype)


def rmsnorm(x, gain, out_dtype, tm=256):
    m, d = x.shape
    tm = min(tm, m)
    return pl.pallas_call(
        _rmsnorm_kernel,
        grid=(m // tm,),
        in_specs=[pl.BlockSpec((tm, d), lambda i: (i, 0)), pl.BlockSpec((1, d), lambda i: (0, 0))],
        out_specs=pl.BlockSpec((tm, d), lambda i: (i, 0)),
        out_shape=jax.ShapeDtypeStruct((m, d), out_dtype),
        compiler_params=_cparams("parallel"),
        name="rmsnorm",
    )(x, gain.reshape(1, d))


def _mm_kernel(a_ref, w_ref, o_ref):
    o_ref[...] = _mm(a_ref[...], w_ref[...])


def matmul(a, w, tm=1024, tn=1024):
    m, k = a.shape
    n = w.shape[1]
    tm, tn = min(tm, m), min(tn, n)
    return pl.pallas_call(
        _mm_kernel,
        grid=(m // tm, n // tn),
        in_specs=[pl.BlockSpec((tm, k), lambda i, j: (i, 0)),
                  pl.BlockSpec((k, tn), lambda i, j: (0, j))],
        out_specs=pl.BlockSpec((tm, tn), lambda i, j: (i, j)),
        out_shape=jax.ShapeDtypeStruct((m, n), F32),
        compiler_params=_cparams("parallel", "parallel"),
        name="matmul",
    )(a, w)


def _lerp_kernel(h_ref, hp_ref, mu_ref, o_ref):
    h = h_ref[...]
    d = hp_ref[...] - h
    for g in range(o_ref.shape[0]):
        o_ref[g] = (h + d * mu_ref[g]).astype(o_ref.dtype)


def token_shift_lerp(h, hp, mu, tm=128):
    m, k = h.shape
    g = mu.shape[0]
    tm = min(tm, m)
    return pl.pallas_call(
        _lerp_kernel,
        grid=(m // tm,),
        in_specs=[pl.BlockSpec((tm, k), lambda i: (i, 0)),
                  pl.BlockSpec((tm, k), lambda i: (i, 0)),
                  pl.BlockSpec((g, 1, k), lambda i: (0, 0, 0))],
        out_specs=pl.BlockSpec((g, tm, k), lambda i: (0, i, 0)),
        out_shape=jax.ShapeDtypeStruct((g, m, k), BF16),
        compiler_params=_cparams("parallel"),
        name="token_shift_lerp",
    )(h, hp, mu.reshape(g, 1, k).astype(F32))


def matmul_grouped(a, w, tile_group, tm=1024, tn=1024):
    _, m, k = a.shape
    n = w.shape[1]
    tm, tn = min(tm, m), min(tn, n)
    assert len(tile_group) == n // tn and list(tile_group) == sorted(tile_group)

    def group(j):
        g = tile_group[0]
        for t in range(1, len(tile_group)):
            if tile_group[t] != tile_group[t - 1]:
                g = g + jnp.where(j >= t, tile_group[t] - tile_group[t - 1], 0)
        return g

    return pl.pallas_call(
        _mm_kernel,
        grid=(m // tm, n // tn),
        in_specs=[pl.BlockSpec((None, tm, k), lambda i, j: (group(j), i, 0)),
                  pl.BlockSpec((k, tn), lambda i, j: (0, j))],
        out_specs=pl.BlockSpec((tm, tn), lambda i, j: (i, j)),
        out_shape=jax.ShapeDtypeStruct((m, n), F32),
        compiler_params=_cparams("parallel", "parallel"),
        name="matmul_grouped",
    )(a, w)


def _outproj_kernel(mix_ref, xo_ref, g_ref, w_ref, x_ref, gain_ref, o_ref, *, k_mix):
    kk = pl.program_id(1)

    @pl.when(kk == 0)
    def _():
        o_ref[...] = jnp.zeros_like(o_ref)

    g = g_ref[...]
    u = jnp.where(kk < k_mix, mix_ref[...], xo_ref[...]) * (g * _sigmoid(g))
    o_ref[...] += _mm(u, w_ref[...])

    @pl.when(kk == pl.num_programs(1) - 1)
    def _():
        y = o_ref[...]
        ms = jnp.mean(y * y, axis=-1, keepdims=True)
        o_ref[...] = x_ref[...] + (y * lax.rsqrt(ms + NORM_EPS)) * gain_ref[...]


def outproj(mix, xo, p, gate_col, w_out, x, gain, tm=512, tk=512):
    m, d_mix = mix.shape
    d_x = xo.shape[1]
    kdim, n = w_out.shape
    tm = min(tm, m)
    assert d_mix % tk == 0 and d_x % tk == 0 and gate_col % tk == 0 and kdim == d_mix + d_x
    k_mix, g_off = d_mix // tk, gate_col // tk
    return pl.pallas_call(
        functools.partial(_outproj_kernel, k_mix=k_mix),
        grid=(m // tm, kdim // tk),
        in_specs=[pl.BlockSpec((tm, tk), lambda i, kk: (i, jnp.minimum(kk, k_mix - 1))),
                  pl.BlockSpec((tm, tk), lambda i, kk: (i, jnp.maximum(kk - k_mix, 0))),
                  pl.BlockSpec((tm, tk), lambda i, kk: (i, g_off + kk)),
                  pl.BlockSpec((tk, n), lambda i, kk: (kk, 0)),
                  pl.BlockSpec((tm, n), lambda i, kk: (i, 0)),
                  pl.BlockSpec((1, n), lambda i, kk: (0, 0))],
        out_specs=pl.BlockSpec((tm, n), lambda i, kk: (i, 0)),
        out_shape=jax.ShapeDtypeStruct((m, n), F32),
        compiler_params=_cparams("parallel", "arbitrary", vmem=VMEM_LIMIT_OUTPROJ),
        name="outproj",
    )(mix, xo, p, w_out, x, gain.reshape(1, n))


def _memattn_kernel(q_ref, k_ref, v_ref, o_ref, *, heads, hd):
    scale = hd ** -0.5
    for h in range(heads):
        cs = slice(h * hd, (h + 1) * hd)
        s = _mm(q_ref[0, :, cs], k_ref[0, :, cs], trans_b=True) * scale
        s = s - jnp.max(s, axis=-1, keepdims=True)
        e = jnp.exp(s)
        p = e / jnp.sum(e, axis=-1, keepdims=True)
        o_ref[0, :, cs] = _mm(p, v_ref[0, :, cs])


def memattn(p, q_col, mk, mv, layer, k_col, v_col, b, t, heads, tq=512):
    n_mem = mk.shape[2]
    w = D_XATTN
    hd = w // heads
    tq = min(tq, t)
    assert q_col % w == 0 and k_col % w == 0 and v_col % w == 0
    q_off, k_off, v_off = q_col // w, k_col // w, v_col // w
    p3 = p.reshape(b, t, p.shape[1])
    out = pl.pallas_call(
        functools.partial(_memattn_kernel, heads=heads, hd=hd),
        grid=(b, t // tq),
        in_specs=[pl.BlockSpec((1, tq, w), lambda i, j: (i, j, q_off)),
                  pl.BlockSpec((None, 1, n_mem, w), lambda i, j: (layer, i, 0, k_off)),
                  pl.BlockSpec((None, 1, n_mem, w), lambda i, j: (layer, i, 0, v_off))],
        out_specs=pl.BlockSpec((1, tq, w), lambda i, j: (i, j, 0)),
        out_shape=jax.ShapeDtypeStruct((b, t, w), F32),
        compiler_params=_cparams("parallel", "parallel"),
        name="memattn",
    )(p3, mk, mv)
    return out.reshape(b * t, w)


def _hgrn_kernel(q_ref, f_ref, i_ref, loglb_ref, l1m_ref, oml_ref, g_ref, s0_ref,
                 o_ref, sout_ref, st_ref, *, L, Hb, nC):
    hd = HGRN_HEAD_DIM
    tstep = pl.program_id(2)

    @pl.when(tstep == 0)
    def _():
        for hh in range(Hb):
            st_ref[hh] = s0_ref[0, hh].T

    row = lax.broadcasted_iota(jnp.int32, (L, hd), 0)
    lane = lax.broadcasted_iota(jnp.int32, (L, hd), 1)

    def gates(c):
        r0 = pl.multiple_of(c * L, L)
        staged = []
        for hh in range(Hb):
            cs = slice(hh * hd, (hh + 1) * hd)
            q = q_ref[0, pl.ds(r0, L), cs]
            f = f_ref[0, pl.ds(r0, L), cs]
            vh = i_ref[0, pl.ds(r0, L), cs]
            qh = q * _sigmoid(q)
            a = loglb_ref[:, cs]
            cc = l1m_ref[:, cs] + _log_sigmoid(f)
            lf = jnp.maximum(a, cc) + jnp.log1p(jnp.exp(-jnp.abs(a - cc)))
            kh = oml_ref[:, cs] * _sigmoid(-f)
            b = _cumsum_rows(lf, L) * LOG2E
            attn = jnp.zeros((L, hd), F32)
            for s in range(L):
                dec = jnp.where(row >= s, jnp.exp2(b - b[s:s + 1, :]), 0.0)
                col = jnp.sum((qh * kh[s:s + 1, :]) * dec, axis=-1, keepdims=True)
                attn = jnp.where(lane == s, col, attn)
            b_last = b[L - 1:L, :]
            staged.append(((qh * jnp.exp2(b)).astype(BF16), attn[:, :L].astype(BF16), vh.astype(BF16),
                           (kh * jnp.exp2(b_last - b)).astype(BF16), jnp.exp2(b_last)))
        return tuple(staged)

    def update(c, staged):
        r0 = pl.multiple_of(c * L, L)
        for hh in range(Hb):
            cs = slice(hh * hd, (hh + 1) * hd)
            qe, attn, vh, kd, e_last = staged[hh]
            st = st_ref[hh]
            o = pl.dot(qe, st.astype(BF16), trans_b=True) + pl.dot(attn, vh)
            o = o * lax.rsqrt(jnp.mean(o * o, axis=-1, keepdims=True) + NORM_EPS)
            o_ref[0, pl.ds(r0, L), cs] = o * g_ref[:, cs]
            st_ref[hh] = st * e_last + pl.dot(vh, kd, trans_a=True)

    def body(c, staged):
        nxt = gates(c + 1)
        update(c, staged)
        return nxt

    update(nC - 1, lax.fori_loop(0, nC - 1, body, gates(0)))

    @pl.when(tstep == pl.num_programs(2) - 1)
    def _():
        for hh in range(Hb):
            sout_ref[0, hh] = st_ref[hh].T


def hgrn_mix(p, b, t, lb, norm_g, s_all, layer, hb):
    heads, hd = s_all.shape[2], s_all.shape[3]
    dm = heads * hd
    L = math.gcd(t, HGRN_CHUNK)
    tb = min(t, SEQ_BLOCK)
    w = hb * hd
    nhb = heads // hb
    p3 = p.reshape(b, t, p.shape[1])
    row = lambda a: a.reshape(1, dm).astype(F32)
    par_spec = pl.BlockSpec((1, w), lambda i, j, tt: (0, j))
    mix, s_out = pl.pallas_call(
        functools.partial(_hgrn_kernel, L=L, Hb=hb, nC=tb // L),
        grid=(b, nhb, t // tb),
        in_specs=[pl.BlockSpec((1, tb, w), lambda i, j, tt: (i, tt, j)),
                  pl.BlockSpec((1, tb, w), lambda i, j, tt: (i, tt, nhb + j)),
                  pl.BlockSpec((1, tb, w), lambda i, j, tt: (i, tt, 2 * nhb + j)),
                  par_spec, par_spec, par_spec, par_spec,
                  pl.BlockSpec((None, 1, hb, hd, hd), lambda i, j, tt: (layer, i, j, 0, 0))],
        out_specs=[pl.BlockSpec((1, tb, w), lambda i, j, tt: (i, tt, j)),
                   pl.BlockSpec((1, hb, hd, hd), lambda i, j, tt: (i, j, 0, 0))],
        out_shape=[jax.ShapeDtypeStruct((b, t, dm), F32), jax.ShapeDtypeStruct(s_all.shape[1:], F32)],
        scratch_shapes=[pltpu.VMEM((hb, hd, hd), F32)],
        compiler_params=_cparams("parallel", "parallel", "arbitrary"),
        name="hgrn_mix",
    )(p3, p3, p3, row(jnp.log(lb)), row(jnp.log1p(-lb)), row(1.0 - lb), row(norm_g), s_all)
    return mix.reshape(b * t, dm), s_out


def _mlstm_kernel(bg_ref, q_ref, k_ref, v_ref, op_ref, ig_ref, fg_ref, g_ref, c0_ref, n0_ref, m0_ref,
                  o_ref, cout_ref, nout_ref, mout_ref, c_sc, n_sc, m_sc, *, L, nC, Hb, heads, dqk, dv):
    dqp = MLSTM_DQK_PAD
    tstep = pl.program_id(2)
    h0 = pl.program_id(1) * Hb
    hs = range(Hb)

    @pl.when(tstep == 0)
    def _():
        c_sc[...] = jnp.zeros_like(c_sc)
        n_sc[...] = jnp.zeros_like(n_sc)
        for hh in hs:
            c_sc[hh, 0:dqk, :] = c0_ref[0, hh]
            n_sc[hh, :, 0:dqk] = n0_ref[0, hh]
            m_sc[hh] = m0_ref[0, hh]

    ti = lax.broadcasted_iota(jnp.int32, (L, L), 0)
    si = lax.broadcasted_iota(jnp.int32, (L, L), 1)
    eye, low = ti == si, si <= ti
    col = lambda r: jnp.sum(jnp.where(eye, r, 0.0), axis=1, keepdims=True)
    kscale = dqk ** -0.5

    def chunk(c, carry):
        r0 = pl.multiple_of(c * L, L)
        q = [q_ref[0, pl.ds(r0, L), hh * dqp:(hh + 1) * dqp] for hh in hs]
        k = [k_ref[0, pl.ds(r0, L), hh * dqp:(hh + 1) * dqp] * kscale for hh in hs]
        v = [v_ref[0, pl.ds(r0, L), hh * dv:(hh + 1) * dv] for hh in hs]
        qb = [x.astype(BF16) for x in q]
        vb = [x.astype(BF16) for x in v]
        s_qk = [pl.dot(qb[hh], k[hh].astype(BF16), trans_b=True) for hh in hs]
        q_c = [pl.dot(qb[hh], c_sc[hh].astype(BF16)) for hh in hs]
        w_inter, a, m_t, b_col, ic_col, m_prev = [], [], [], [], [], []
        for hh in hs:
            ic_row = ig_ref[0, hh, pl.ds(c, 1), :] + bg_ref[h0 + hh]
            lf_row = _log_sigmoid(fg_ref[0, hh, pl.ds(c, 1), :] + bg_ref[heads + h0 + hh])
            bc = jnp.sum(jnp.where(low, lf_row, 0.0), axis=1, keepdims=True)
            b_row = jnp.sum(jnp.where(ti <= si, col(lf_row), 0.0), axis=0, keepdims=True)
            log_w = jnp.where(low, bc - b_row + ic_row, -jnp.inf)
            m = m_sc[hh]
            log_inter = bc + m
            mt = jnp.maximum(log_inter, jnp.max(log_w, axis=1, keepdims=True))
            w_inter.append(jnp.exp(log_inter - mt))
            a.append(jnp.exp(log_w - mt) * s_qk[hh])
            m_t.append(mt)
            b_col.append(bc)
            ic_col.append(col(ic_row))
            m_prev.append(m)
        a_v = [pl.dot(a[hh].astype(BF16), vb[hh]) for hh in hs]
        kw, decay = [], []
        for hh in hs:
            m_last = m_t[hh][L - 1:L, :]
            b_last = b_col[hh][L - 1:L, :]
            kw.append(jnp.exp(b_last - b_col[hh] + ic_col[hh] - m_last) * k[hh])
            decay.append(jnp.exp(b_last + m_prev[hh] - m_last))
            m_sc[hh] = m_last
        kw_v = [pl.dot(kw[hh].astype(BF16), vb[hh], trans_a=True) for hh in hs]
        for hh in hs:
            n = n_sc[hh]
            num = w_inter[hh] * q_c[hh] + a_v[hh]
            den = w_inter[hh] * jnp.sum(q[hh] * n, axis=1, keepdims=True) + jnp.sum(a[hh], axis=1, keepdims=True)
            hid = num / jnp.maximum(jnp.abs(den), jnp.exp(-m_t[hh]))
            c_sc[hh] = decay[hh] * c_sc[hh] + kw_v[hh]
            n_sc[hh] = decay[hh] * n + jnp.sum(kw[hh], axis=0, keepdims=True)
            vs = slice(hh * dv, (hh + 1) * dv)
            hn = hid * lax.rsqrt(jnp.mean(hid * hid, axis=1, keepdims=True) + NORM_EPS) * g_ref[:, vs]
            o_ref[0, pl.ds(r0, L), vs] = _sigmoid(op_ref[0, pl.ds(r0, L), vs]) * hn
        return carry

    lax.fori_loop(0, nC, chunk, 0)

    @pl.when(tstep == pl.num_programs(2) - 1)
    def _():
        for hh in hs:
            cout_ref[0, hh] = c_sc[hh, 0:dqk, :]
            nout_ref[0, hh] = n_sc[hh, :, 0:dqk]
            mout_ref[0, hh] = m_sc[hh]


def mlstm_mix(p, gates, b, t, b_gate, norm_g, c_all, n_all, m_all, layer, hb=4):
    heads, dqk, dv = c_all.shape[2], c_all.shape[3], c_all.shape[4]
    dqp = MLSTM_DQK_PAD
    dm = heads * dv
    L = math.gcd(t, MLSTM_CHUNK)
    tb = min(t, SEQ_BLOCK)
    nC, nc_step = t // L, tb // L
    nhb = heads // hb
    p3 = p.reshape(b, t, p.shape[1])
    g3 = gates.reshape(b, t, 2 * heads)
    ig = jnp.swapaxes(g3[..., :heads], 1, 2).reshape(b, heads, nC, L)
    fg = jnp.swapaxes(g3[..., heads:], 1, 2).reshape(b, heads, nC, L)
    q_off, k_off = 2 * dm // (hb * dqp), (2 * dm + heads * dqp) // (hb * dqp)
    seq = lambda width, off: pl.BlockSpec((1, tb, hb * width), lambda i, j, tt: (i, tt, off + j))
    gate_spec = pl.BlockSpec((1, hb, nc_step, L), lambda i, j, tt: (i, j, tt, 0))
    c_in = pl.BlockSpec((None, 1, hb, dqk, dv), lambda i, j, tt: (layer, i, j, 0, 0))
    n_in = pl.BlockSpec((None, 1, hb, 1, dqk), lambda i, j, tt: (layer, i, j, 0, 0))
    m_in = pl.BlockSpec((None, 1, hb, 1, 1), lambda i, j, tt: (layer, i, j, 0, 0))
    c_out = pl.BlockSpec((1, hb, dqk, dv), lambda i, j, tt: (i, j, 0, 0))
    n_out = pl.BlockSpec((1, hb, 1, dqk), lambda i, j, tt: (i, j, 0, 0))
    m_out = pl.BlockSpec((1, hb, 1, 1), lambda i, j, tt: (i, j, 0, 0))
    mix, c, n, m = pl.pallas_call(
        functools.partial(_mlstm_kernel, L=L, nC=nc_step, Hb=hb, heads=heads, dqk=dqk, dv=dv),
        grid=(b, nhb, t // tb),
        in_specs=[pl.BlockSpec(memory_space=pltpu.SMEM),
                  seq(dqp, q_off), seq(dqp, k_off), seq(dv, 0), seq(dv, nhb),
                  gate_spec, gate_spec,
                  pl.BlockSpec((1, hb * dv), lambda i, j, tt: (0, j)),
                  c_in, n_in, m_in],
        out_specs=[seq(dv, 0), c_out, n_out, m_out],
        out_shape=[jax.ShapeDtypeStruct((b, t, dm), F32),
                   jax.ShapeDtypeStruct((b, heads, dqk, dv), F32),
                   jax.ShapeDtypeStruct((b, heads, 1, dqk), F32),
                   jax.ShapeDtypeStruct((b, heads, 1, 1), F32)],
        scratch_shapes=[pltpu.VMEM((hb, dqp, dv), F32), pltpu.VMEM((hb, 1, dqp), F32), pltpu.VMEM((hb, 1, 1), F32)],
        compiler_params=_cparams("parallel", "parallel", "arbitrary"),
        name="mlstm_mix",
    )(b_gate.astype(F32), p3, p3, p3, p3, ig, fg, norm_g.reshape(1, dm).astype(F32),
      c_all, n_all.reshape(n_all.shape[:3] + (1, dqk)), m_all.reshape(m_all.shape[:3] + (1, 1)))
    return mix.reshape(b * t, dm), c[None], n.reshape(1, b, heads, dqk), m.reshape(1, b, heads)


def _seg_sum(x, lane_lo):
    s0 = jnp.sum(jnp.where(lane_lo, x, 0.0), axis=-1, keepdims=True)
    s1 = jnp.sum(jnp.where(lane_lo, 0.0, x), axis=-1, keepdims=True)
    return jnp.where(lane_lo, s0, s1)


def _rwkv_prep_kernel(k_ref, wlo_ref, alo_ref, w2_ref, a2_ref, w0_ref, a0_ref, kk_ref, ka_ref,
                      ld_ref, lc_ref, km_ref, av_ref, bv_ref, *, L):
    k = k_ref[...]
    lw = w0_ref[...] + _mm(jnp.tanh(wlo_ref[...]), w2_ref[...])
    log_w = -(jnp.maximum(-lw, 0.0) + jnp.log1p(jnp.exp(-jnp.abs(lw)))) - 0.5
    ld = -jnp.exp(log_w)
    ld_ref[...] = ld
    lc_ref[...] = _cumsum_rows(ld, L)
    a = _sigmoid(a0_ref[...] + _mm(alo_ref[...], a2_ref[...]))
    kk = k * kk_ref[...]
    tn = k.shape[1]
    lane_lo = lax.broadcasted_iota(jnp.int32, (k.shape[0], LANES), 1) < RWKV_HEAD_DIM
    for c in range(tn // LANES):
        cs = slice(c * LANES, (c + 1) * LANES)
        kc = kk[:, cs]
        kc = kc / jnp.maximum(jnp.sqrt(_seg_sum(kc * kc, lane_lo)), 1e-12)
        av_ref[:, cs] = -kc
        bv_ref[:, cs] = kc * a[:, cs]
    km_ref[...] = k * (1.0 + (a - 1.0) * ka_ref[...])


def rwkv_prep(p, k_col, lora, w2, a2, w0, a0, k_k, k_a, L, tm=512, tn=512):
    m = p.shape[0]
    dm = w2.shape[1]
    lp = RWKV_LORA_PAD
    tm = min(tm, m)
    assert tm % L == 0
    k_off = k_col // tn
    par = pl.BlockSpec((1, tn), lambda i, j: (0, j))
    out = pl.BlockSpec((tm, tn), lambda i, j: (i, j))
    row = lambda a: a.reshape(1, dm).astype(F32)
    return pl.pallas_call(
        functools.partial(_rwkv_prep_kernel, L=L),
        grid=(m // tm, dm // tn),
        in_specs=[pl.BlockSpec((tm, tn), lambda i, j: (i, k_off + j)),
                  pl.BlockSpec((tm, lp), lambda i, j: (i, 0)),
                  pl.BlockSpec((tm, lp), lambda i, j: (i, 1)),
                  pl.BlockSpec((lp, tn), lambda i, j: (0, j)),
                  pl.BlockSpec((lp, tn), lambda i, j: (0, j)),
                  par, par, par, par],
        out_specs=[out] * 5,
        out_shape=[jax.ShapeDtypeStruct((m, dm), F32)] * 5,
        compiler_params=_cparams("parallel", "parallel"),
        name="rwkv_prep",
    )(p, lora, lora, w2, a2, row(w0), row(a0), row(k_k), row(k_a))


def _rwkv_kernel(r_ref, ld_ref, lc_ref, k_ref, v_ref, a_ref, b_ref, rk_ref, lg_ref, lb_ref, s0_ref,
                 y_ref, sout_ref, sp_ref, *, L, Gp, Ns, nC):
    hd = RWKV_HEAD_DIM
    G = 2 * Gp
    R = G * L
    W = Gp * LANES
    tstep = pl.program_id(2)
    bm = (lax.broadcasted_iota(jnp.int32, (LANES, LANES), 0) // hd
          == lax.broadcasted_iota(jnp.int32, (LANES, LANES), 1) // hd)

    @pl.when(tstep == 0)
    def _():
        for p in range(Ns * Gp):
            x = s0_ref[0, p]
            sp_ref[p] = jnp.where(bm, jnp.concatenate([x, x], axis=1), 0.0)

    head_mask = (lax.broadcasted_iota(jnp.int32, (R, W), 0) // L
                 == lax.broadcasted_iota(jnp.int32, (R, W), 1) // hd)
    ri = lax.broadcasted_iota(jnp.int32, (R, R), 0)
    ci = lax.broadcasted_iota(jnp.int32, (R, R), 1)
    same = (ri // L) == (ci // L)
    strict = same & (ci < ri)
    ri2 = lax.broadcasted_iota(jnp.int32, (R, 2 * R), 0)
    ci2 = lax.broadcasted_iota(jnp.int32, (R, 2 * R), 1)
    ci2 = jnp.where(ci2 >= R, ci2 - R, ci2)
    incl2 = ((ri2 // L) == (ci2 // L)) & (ci2 <= ri2)
    lo = (lax.broadcasted_iota(jnp.int32, (L, LANES), 1) < hd)
    n_dbl = max(1, (L - 1).bit_length())
    stack = lambda x: jnp.where(head_mask, jnp.concatenate([x] * G, axis=0), 0.0).astype(BF16)

    streams = range(Ns)

    def chunk(c, carry):
        r0 = pl.multiple_of(c * L, L)
        ld_ = lambda ref, s: ref[0, pl.ds(r0, L), s * W:(s + 1) * W]
        ar2, bk2, v2, bkc, c_last = [], [], [], [], []
        for s in streams:
            r, ldec, lc, k, v, av, bv = (ld_(r_ref, s), ld_(ld_ref, s), ld_(lc_ref, s), ld_(k_ref, s),
                                         ld_(v_ref, s), ld_(a_ref, s), ld_(b_ref, s))
            lcl = lc[L - 1:L, :]
            e_neg = jnp.exp(-lc)
            e_rem = jnp.exp(lcl - lc)
            ar2.append(jnp.concatenate([stack(av * jnp.exp(lc - ldec)), stack(r * jnp.exp(lc))], axis=0))
            bk2.append(jnp.concatenate([stack(bv * e_neg), stack(k * e_neg)], axis=0))
            v2.append(stack(v))
            bkc.append(jnp.concatenate([stack(bv * e_rem), stack(k * e_rem)], axis=0))
            c_last.append(jnp.exp(lcl))
        g = [pl.dot(ar2[s], bk2[s], trans_b=True) for s in streams]
        nmat = [jnp.where(strict, g[s][:R, :R], 0.0).astype(BF16) for s in streams]
        a_ak = [jnp.where(strict, g[s][:R, R:], 0.0).astype(BF16) for s in streams]
        a_rbk = [jnp.where(incl2, g[s][R:, :], 0.0).astype(BF16) for s in streams]
        akv = [pl.dot(a_ak[s], v2[s]) for s in streams]
        minv = [nmat[s].astype(F32) for s in streams]
        for it in range(1, n_dbl):
            nmat = [pl.dot(nmat[s], nmat[s]).astype(BF16) for s in streams]
            minv = [minv[s] + nmat[s] + pl.dot(nmat[s], minv[s].astype(BF16)) for s in streams]
        ur0 = [jnp.concatenate(
            [pl.dot(ar2[s][:, p * LANES:(p + 1) * LANES], sp_ref[s * Gp + p].astype(BF16), trans_b=True)
             for p in range(Gp)], axis=1) for s in streams]
        x = [ur0[s][:R] + akv[s] for s in streams]
        x = [x[s] + pl.dot(minv[s].astype(BF16), x[s].astype(BF16)) for s in streams]
        xv = [jnp.concatenate([x[s].astype(BF16), v2[s]], axis=0) for s in streams]
        y2 = [ur0[s][R:] + pl.dot(a_rbk[s], xv[s]) for s in streams]
        for s in streams:
            y = y2[s][0:L]
            for j in range(1, G):
                y = y + y2[s][j * L:(j + 1) * L]
            for p in range(Gp):
                cs = slice(p * LANES, (p + 1) * LANES)
                gs = slice(s * W + p * LANES, s * W + (p + 1) * LANES)
                q = s * Gp + p
                sp_ref[q] = sp_ref[q] * c_last[s][:, cs] + pl.dot(xv[s][:, cs], bkc[s][:, cs], trans_a=True)
                yp = y[:, cs]
                rp, kp, vp = r_ref[0, pl.ds(r0, L), gs], k_ref[0, pl.ds(r0, L), gs], v_ref[0, pl.ds(r0, L), gs]
                mean = _seg_sum(yp, lo) * (1.0 / hd)
                yc = yp - mean
                var = _seg_sum(yc * yc, lo) * (1.0 / hd)
                yn = yc * lax.rsqrt(var + RWKV_GN_EPS) * lg_ref[:, gs] + lb_ref[:, gs]
                bonus = _seg_sum(rp * kp * rk_ref[:, gs], lo) * vp
                y_ref[0, pl.ds(r0, L), gs] = yn + bonus
        return carry

    lax.fori_loop(0, nC, chunk, 0)

    @pl.when(tstep == pl.num_programs(2) - 1)
    def _():
        row_lo = lax.broadcasted_iota(jnp.int32, (LANES, hd), 0) < hd
        for p in range(Ns * Gp):
            sp = sp_ref[p]
            sout_ref[0, p] = jnp.where(row_lo, sp[:, :hd], sp[:, hd:])


def rwkv_mix(r_src, r_col, v_col, ld, lc, km, av, bv, b, t, L, r_k, ln_g, ln_b, s0):
    heads, hd = s0.shape[1], s0.shape[2]
    dm = heads * hd
    pairs = heads // 2
    gp = max(g for g in range(1, pairs + 1)
             if pairs % g == 0 and 2 * g * L <= RWKV_ROWS and (2 * g * L) % LANES == 0)
    ns = max(n for n in range(1, RWKV_STREAMS + 1) if (pairs // gp) % n == 0)
    tb = min(t, RWKV_SEQ_BLOCK)
    w = ns * gp * LANES
    r_off, v_off = r_col // w, v_col // w
    three = lambda a: a.reshape(b, t, a.shape[1])
    row = lambda a: a.reshape(1, dm).astype(F32)
    seq = lambda off: pl.BlockSpec((1, tb, w), lambda i, j, tt: (i, tt, off + j))
    par = pl.BlockSpec((1, w), lambda i, j, tt: (0, j))
    st_spec = pl.BlockSpec((1, ns * gp, LANES, hd), lambda i, j, tt: (i, j, 0, 0))
    s0p = s0.reshape(b, pairs, LANES, hd)
    y, s_out = pl.pallas_call(
        functools.partial(_rwkv_kernel, L=L, Gp=gp, Ns=ns, nC=tb // L),
        grid=(b, pairs // (gp * ns), t // tb),
        in_specs=[seq(r_off), seq(0), seq(0), seq(0), seq(v_off), seq(0), seq(0), par, par, par, st_spec],
        out_specs=[seq(0), st_spec],
        out_shape=[jax.ShapeDtypeStruct((b, t, dm), F32), jax.ShapeDtypeStruct(s0p.shape, F32)],
        scratch_shapes=[pltpu.VMEM((ns * gp, LANES, LANES), F32)],
        compiler_params=_cparams("parallel", "parallel", "arbitrary"),
        name="rwkv_mix",
    )(three(r_src), three(ld), three(lc), three(km), three(r_src), three(av), three(bv),
      row(r_k), row(ln_g), row(ln_b), s0p)
    return y.reshape(b * t, dm), s_out.reshape(s0.shape)


def _hgrn_lower_bounds(logits):
    p = jax.nn.softmax(logits.astype(F32), axis=0)
    return jnp.maximum(jnp.cumsum(p, axis=0) - p[0], 0.0)


def _mlstm_weights(w_in):
    dq = MLSTM_HEADS * MLSTM_DQK
    o = [0, dq, 2 * dq, 2 * dq + D_MIX, 2 * dq + D_MIX + MLSTM_HEADS, 2 * dq + D_MIX + 2 * MLSTM_HEADS]
    q, k, v = w_in[:, o[0]:o[1]], w_in[:, o[1]:o[2]], w_in[:, o[2]:o[3]]
    gates = w_in[:, o[3]:o[5]]
    rest = w_in[:, o[5]:]
    pad = lambda a: jnp.pad(a.reshape(-1, MLSTM_HEADS, MLSTM_DQK),
                            ((0, 0), (0, 0), (0, MLSTM_DQK_PAD - MLSTM_DQK))).reshape(-1, MLSTM_HEADS * MLSTM_DQK_PAD)
    main = jnp.concatenate([v, rest[:, :D_MIX], pad(q), pad(k), rest[:, D_MIX:]], axis=1).astype(BF16)
    gates = jnp.pad(gates, ((0, 0), (0, LANES - 2 * MLSTM_HEADS))).astype(BF16)
    return main, gates


def _rwkv_weights(w_in, mu):
    c = [0]
    for n in (D_MIX, RWKV_LORA, D_MIX, D_MIX, RWKV_LORA, D_GATE, D_XATTN):
        c.append(c[-1] + n)
    sl = lambda i: w_in[:, c[i]:c[i + 1]]
    main = jnp.concatenate([sl(0), sl(2), sl(3), sl(5), sl(6)], axis=1).astype(BF16)
    padl = lambda a: jnp.pad(a, ((0, 0), (0, RWKV_LORA_PAD - RWKV_LORA)))
    lora = jnp.concatenate([padl(sl(1)), padl(sl(4))], axis=1).astype(BF16)
    mu_ext = jnp.concatenate([mu.astype(F32), jnp.zeros((1, D_MODEL), F32)], axis=0)
    tile = 1024
    main_groups = ([0] * (D_MIX // tile) + [2] * (D_MIX // tile) + [3] * (D_MIX // tile)
                   + [5] * (D_GATE // tile) + [6] * (D_XATTN // tile))
    return main, lora, mu_ext, main_groups, [1, 4]


def _trunk(x3, mk, mv, mem_cols, s_hgrn, s_mc, s_mn, s_mm, s_rwkv, s_shift, prm, hgrn_hb):
    b, t, d = x3.shape
    x = x3.reshape(b * t, d)
    out = dict(hgrn=[], mc=[], mn=[], mm=[], rwkv=[], shift=[])
    for i in range(DEPTH):
        kind, j = i % N_MIXERS, i // N_MIXERS
        if kind == 0:
            h = rmsnorm(x, prm['norm_pre'][i], BF16)
            p = matmul(h, prm['hgrn_w'][j])
            mix, s = hgrn_mix(p, b, t, prm['hgrn_lb'][j], prm['hgrn_norm'][j], s_hgrn, j, hgrn_hb)
            out['hgrn'].append(s)
            gate_col, q_col = 3 * D_MIX, 3 * D_MIX + D_GATE
        elif kind == 1:
            h = rmsnorm(x, prm['norm_pre'][i], BF16)
            w_main, w_gates = prm['mlstm_w'][j]
            p = matmul(h, w_main)
            gates = matmul(h, w_gates, tn=LANES)[:, :2 * MLSTM_HEADS]
            mix, c, n, m = mlstm_mix(p, gates, b, t, prm['mlstm_b_gate'][j], prm['mlstm_norm'][j],
                                     s_mc, s_mn, s_mm, j)
            out['mc'].append(c)
            out['mn'].append(n)
            out['mm'].append(m)
            gate_col = 2 * D_MIX + 2 * MLSTM_HEADS * MLSTM_DQK_PAD
            q_col = gate_col + D_GATE
        else:
            h = rmsnorm(x, prm['norm_pre'][i], F32)
            h3 = h.reshape(b, t, d)
            hp = jnp.concatenate([s_shift[j][:, None, :], h3[:, :-1]], axis=1).reshape(b * t, d)
            w_main, w_lora, mu_ext, main_groups, lora_groups = prm['rwkv_w'][j]
            mixes = token_shift_lerp(h, hp, mu_ext)
            p = matmul_grouped(mixes, w_main, main_groups)
            lora = matmul_grouped(mixes, w_lora, lora_groups, tn=RWKV_LORA_PAD)
            L = math.gcd(t, RWKV_CHUNK)
            ld, lc, km, av, bv = rwkv_prep(p, D_MIX, lora, prm['rwkv_w2'][j], prm['rwkv_a2'][j], prm['rwkv_w0'][j],
                                           prm['rwkv_a0'][j], prm['rwkv_k_k'][j], prm['rwkv_k_a'][j], L)
            mix, s = rwkv_mix(p, 0, 2 * D_MIX, ld, lc, km, av, bv, b, t, L, prm['rwkv_r_k'][j],
                              prm['rwkv_ln_g'][j], prm['rwkv_ln_b'][j], s_rwkv[j])
            out['rwkv'].append(s)
            out['shift'].append(h3[:, -1])
            gate_col, q_col = 3 * D_MIX, 3 * D_MIX + D_GATE
        lyr, k_col, v_col = mem_cols(i)
        xo = memattn(p, q_col, mk[i] if isinstance(mk, list) else mk, mv[i] if isinstance(mv, list) else mv,
                     lyr, k_col, v_col, b, t, X_HEADS)
        x = outproj(mix, xo, p, gate_col, prm['w_out'][i], x, prm['norm_post'][i])
    cat = lambda parts: parts[0] if len(parts) == 1 else jnp.concatenate(parts, axis=0)
    return (x.reshape(b, t, d), jnp.stack(out['hgrn']), cat(out['mc']), cat(out['mn']), cat(out['mm']),
            jnp.stack(out['rwkv']), jnp.stack(out['shift']))


def kernel(x_prompt, x_sample, mem_prompt, cache_mem_k, cache_mem_v, state_hgrn, state_mlstm_c, state_mlstm_n, state_mlstm_m, state_rwkv, state_rwkv_shift, norm_pre, norm_post, norm_mem, w_mem_kv, w_out, hgrn_w_in, hgrn_lb_logits, hgrn_norm, mlstm_w_in, mlstm_b_gate, mlstm_norm, rwkv_w_in, rwkv_mu, rwkv_w0, rwkv_w2, rwkv_a0, rwkv_a2, rwkv_k_k, rwkv_k_a, rwkv_r_k, rwkv_ln_g, rwkv_ln_b):
    n_h, n_m, n_r = hgrn_w_in.shape[0], mlstm_w_in.shape[0], rwkv_w_in.shape[0]
    pad_lora = lambda a: jnp.pad(a, ((0, RWKV_LORA_PAD - RWKV_LORA), (0, 0))).astype(BF16)
    prm = dict(
        norm_pre=norm_pre, norm_post=norm_post, w_out=w_out.astype(BF16),
        hgrn_w=[hgrn_w_in[j].astype(BF16) for j in range(n_h)],
        hgrn_lb=_hgrn_lower_bounds(hgrn_lb_logits), hgrn_norm=hgrn_norm,
        mlstm_w=[_mlstm_weights(mlstm_w_in[j]) for j in range(n_m)],
        mlstm_b_gate=mlstm_b_gate, mlstm_norm=mlstm_norm,
        rwkv_w=[_rwkv_weights(rwkv_w_in[j], rwkv_mu[j]) for j in range(n_r)],
        rwkv_w0=rwkv_w0, rwkv_w2=[pad_lora(rwkv_w2[j]) for j in range(n_r)],
        rwkv_a0=rwkv_a0, rwkv_a2=[pad_lora(rwkv_a2[j]) for j in range(n_r)],
        rwkv_k_k=rwkv_k_k, rwkv_k_a=rwkv_k_a, rwkv_r_k=rwkv_r_k, rwkv_ln_g=rwkv_ln_g, rwkv_ln_b=rwkv_ln_b)

    bp, n_mem, d = mem_prompt.shape
    mem2 = mem_prompt.reshape(bp * n_mem, d)
    kvs = [matmul(rmsnorm(mem2, norm_mem[i], BF16), w_mem_kv[i].astype(BF16)).reshape(1, bp, n_mem, 2 * D_XATTN)
           for i in range(DEPTH)]
    heads_shape = (bp, n_mem, X_HEADS, X_HEAD_DIM)
    mem_k_prompt = jnp.stack([kv[0, :, :, :D_XATTN].reshape(heads_shape) for kv in kvs])
    mem_v_prompt = jnp.stack([kv[0, :, :, D_XATTN:].reshape(heads_shape) for kv in kvs])

    dt = x_prompt.dtype
    zeros = lambda *shape: jnp.zeros(shape, dt)
    (y_prompt, hgrn_p, mc_p, mn_p, mm_p, rwkv_p, shift_p) = _trunk(
        x_prompt, kvs, kvs, lambda i: (0, 0, D_XATTN),
        zeros(n_h, bp, HGRN_HEADS, HGRN_HEAD_DIM, HGRN_HEAD_DIM),
        zeros(n_m, bp, MLSTM_HEADS, MLSTM_DQK, MLSTM_DV),
        zeros(n_m, bp, MLSTM_HEADS, MLSTM_DQK),
        zeros(n_m, bp, MLSTM_HEADS),
        zeros(n_r, bp, RWKV_HEADS, RWKV_HEAD_DIM, RWKV_HEAD_DIM),
        zeros(n_r, bp, D_MODEL),
        prm, hgrn_hb=6)
    bs = x_sample.shape[0]
    ck = cache_mem_k.reshape(DEPTH, bs, n_mem, D_XATTN)
    cv = cache_mem_v.reshape(DEPTH, bs, n_mem, D_XATTN)
    (y_sample, hgrn_s, mc_s, mn_s, mm_s, rwkv_s, shift_s) = _trunk(
        x_sample, ck, cv, lambda i: (i, 0, 0), state_hgrn, state_mlstm_c, state_mlstm_n, state_mlstm_m,
        state_rwkv, state_rwkv_shift, prm, hgrn_hb=8)
    return (y_prompt, y_sample, mem_k_prompt, mem_v_prompt,
            hgrn_p, mc_p, mn_p, mm_p, rwkv_p, shift_p,
            hgrn_s, mc_s, mn_s, mm_s, rwkv_s, shift_s)
```

```python
import functools
import math

import jax
import jax.numpy as jnp
from jax import lax
from jax.experimental import pallas as pl
from jax.experimental.pallas import tpu as pltpu

F32 = jnp.float32
BF16 = jnp.bfloat16

D_MODEL = 4096
DEPTH = 4
N_MIXERS = 3
D_MIX = 3 * D_MODEL // 4
D_XATTN = D_MODEL // 4
D_GATE = D_MIX + D_XATTN
N_MEM = 256
X_HEADS = 4
X_HEAD_DIM = D_XATTN // X_HEADS

HGRN_HEAD_DIM = 128
HGRN_HEADS = D_MIX // HGRN_HEAD_DIM
HGRN_CHUNK = 16

MLSTM_HEADS = 8
MLSTM_DV = D_MIX // MLSTM_HEADS
MLSTM_DQK = MLSTM_DV // 2
MLSTM_DQK_PAD = 256
MLSTM_CHUNK = 64

RWKV_HEAD_DIM = 64
RWKV_HEADS = D_MIX // RWKV_HEAD_DIM
RWKV_LORA = max(32, int(round(1.8 * math.sqrt(D_MIX) / 32)) * 32)
RWKV_LORA_PAD = 128
RWKV_GN_EPS = 64e-5
RWKV_CHUNK = 64
RWKV_ROWS = 256
RWKV_STREAMS = 3
RWKV_SEQ_BLOCK = 512

NORM_EPS = 1e-6
LOG2E = 1.4426950408889634
LANES = 128
SEQ_BLOCK = 512
VMEM_LIMIT = 48 * 1024 * 1024
VMEM_LIMIT_OUTPROJ = 58 * 1024 * 1024


def _cparams(*sem, vmem=VMEM_LIMIT):
    return pltpu.CompilerParams(dimension_semantics=sem, vmem_limit_bytes=vmem)


def _mm(a, b, **kw):
    return pl.dot(a.astype(BF16), b.astype(BF16), **kw)


def _sigmoid(x):
    return 1.0 / (1.0 + jnp.exp(-x))


def _silu(g):
    return g * _sigmoid(g)


def _log_sigmoid(x):
    return jnp.minimum(x, 0.0) - jnp.log1p(jnp.exp(-jnp.abs(x)))


def _cumsum_rows(x, block):
    row = lax.broadcasted_iota(jnp.int32, x.shape, 0) & (block - 1)
    sh = 1
    while sh < block:
        x = x + jnp.where(row >= sh, pltpu.roll(x, sh, axis=0), 0.0)
        sh *= 2
    return x


def _rmsnorm_kernel(x_ref, g_ref, o_ref):
    x = x_ref[...]
    ms = jnp.mean(x * x, axis=-1, keepdims=True)
    o_ref[...] = ((x * lax.rsqrt(ms + NORM_EPS)) * g_ref[...]).astype(o_ref.dtype)


def rmsnorm(x, gain, out_dtype, tm=256):
    m, d = x.shape
    tm = min(tm, m)
    return pl.pallas_call(
        _rmsnorm_kernel,
        grid=(m // tm,),
        in_specs=[pl.BlockSpec((tm, d), lambda i: (i, 0)), pl.BlockSpec((1, d), lambda i: (0, 0))],
        out_specs=pl.BlockSpec((tm, d), lambda i: (i, 0)),
        out_shape=jax.ShapeDtypeStruct((m, d), out_dtype),
        compiler_params=_cparams("parallel"),
        name="rmsnorm",
    )(x, gain.reshape(1, d))


def _mm_kernel(a_ref, w_ref, o_ref):
    o_ref[...] = _mm(a_ref[...], w_ref[...])


def matmul(a, w, layer=None, tm=1024, tn=1024):
    m, k = a.shape
    n = w.shape[-1]
    tm, tn = min(tm, m), min(tn, n)
    if layer is None:
        w_spec = pl.BlockSpec((k, tn), lambda i, j: (0, j))
    else:
        w_spec = pl.BlockSpec((None, k, tn), lambda i, j: (layer, 0, j))
    return pl.pallas_call(
        _mm_kernel,
        grid=(m // tm, n // tn),
        in_specs=[pl.BlockSpec((tm, k), lambda i, j: (i, 0)), w_spec],
        out_specs=pl.BlockSpec((tm, tn), lambda i, j: (i, j)),
        out_shape=jax.ShapeDtypeStruct((m, n), F32),
        compiler_params=_cparams("parallel", "parallel"),
        name="matmul",
    )(a, w)


def _lerp_kernel(h_ref, hp_ref, mu_ref, o_ref):
    h = h_ref[...]
    d = hp_ref[...] - h
    for g in range(o_ref.shape[0]):
        o_ref[g] = (h + d * mu_ref[g]).astype(o_ref.dtype)


def token_shift_lerp(h, hp, mu, tm=128):
    m, k = h.shape
    g = mu.shape[0]
    tm = min(tm, m)
    return pl.pallas_call(
        _lerp_kernel,
        grid=(m // tm,),
        in_specs=[pl.BlockSpec((tm, k), lambda i: (i, 0)),
                  pl.BlockSpec((tm, k), lambda i: (i, 0)),
                  pl.BlockSpec((g, 1, k), lambda i: (0, 0, 0))],
        out_specs=pl.BlockSpec((g, tm, k), lambda i: (0, i, 0)),
        out_shape=jax.ShapeDtypeStruct((g, m, k), BF16),
        compiler_params=_cparams("parallel"),
        name="token_shift_lerp",
    )(h, hp, mu.reshape(g, 1, k).astype(F32))


def matmul_grouped(a, w, tile_group, tm=1024, tn=1024):
    _, m, k = a.shape
    n = w.shape[1]
    tm, tn = min(tm, m), min(tn, n)
    assert len(tile_group) == n // tn and list(tile_group) == sorted(tile_group)

    def group(j):
        g = tile_group[0]
        for t in range(1, len(tile_group)):
            if tile_group[t] != tile_group[t - 1]:
                g = g + jnp.where(j >= t, tile_group[t] - tile_group[t - 1], 0)
        return g

    return pl.pallas_call(
        _mm_kernel,
        grid=(m // tm, n // tn),
        in_specs=[pl.BlockSpec((None, tm, k), lambda i, j: (group(j), i, 0)),
                  pl.BlockSpec((k, tn), lambda i, j: (0, j))],
        out_specs=pl.BlockSpec((tm, tn), lambda i, j: (i, j)),
        out_shape=jax.ShapeDtypeStruct((m, n), F32),
        compiler_params=_cparams("parallel", "parallel"),
        name="matmul_grouped",
    )(a, w)


def _outproj_kernel(mix_ref, xo_ref, w_ref, x_ref, gain_ref, o_ref, *, k_mix):
    kk = pl.program_id(1)

    @pl.when(kk == 0)
    def _():
        o_ref[...] = jnp.zeros_like(o_ref)

    u = jnp.where(kk < k_mix, mix_ref[...], xo_ref[...])
    o_ref[...] += pl.dot(u, w_ref[...])

    @pl.when(kk == pl.num_programs(1) - 1)
    def _():
        y = o_ref[...]
        ms = jnp.mean(y * y, axis=-1, keepdims=True)
        o_ref[...] = x_ref[...] + (y * lax.rsqrt(ms + NORM_EPS)) * gain_ref[...]


def outproj(u_mix, u_xo, w_all, layer, x, gain, tm=512, tk=512):
    m, d_mix = u_mix.shape
    d_x = u_xo.shape[1]
    kdim, n = w_all.shape[1:]
    tm = min(tm, m)
    assert d_mix % tk == 0 and d_x % tk == 0 and kdim == d_mix + d_x
    k_mix = d_mix // tk
    return pl.pallas_call(
        functools.partial(_outproj_kernel, k_mix=k_mix),
        grid=(m // tm, kdim // tk),
        in_specs=[pl.BlockSpec((tm, tk), lambda i, kk: (i, jnp.minimum(kk, k_mix - 1))),
                  pl.BlockSpec((tm, tk), lambda i, kk: (i, jnp.maximum(kk - k_mix, 0))),
                  pl.BlockSpec((None, tk, n), lambda i, kk: (layer, kk, 0)),
                  pl.BlockSpec((tm, n), lambda i, kk: (i, 0)),
                  pl.BlockSpec((1, n), lambda i, kk: (0, 0))],
        out_specs=pl.BlockSpec((tm, n), lambda i, kk: (i, 0)),
        out_shape=jax.ShapeDtypeStruct((m, n), F32),
        compiler_params=_cparams("parallel", "arbitrary", vmem=VMEM_LIMIT_OUTPROJ),
        name="outproj",
    )(u_mix, u_xo, w_all, x, gain.reshape(1, n))


def _memattn_kernel(q_ref, g_ref, k_ref, v_ref, o_ref, *, heads, hd, bb):
    scale = hd ** -0.5
    for i in range(bb):
        for h in range(heads):
            cs = slice(h * hd, (h + 1) * hd)
            s = _mm(q_ref[i, :, cs], k_ref[i, :, cs], trans_b=True) * scale
            s = s - jnp.max(s, axis=-1, keepdims=True)
            e = jnp.exp(s)
            p = e / jnp.sum(e, axis=-1, keepdims=True)
            o_ref[i, :, cs] = (_mm(p, v_ref[i, :, cs]) * _silu(g_ref[i, :, cs])).astype(o_ref.dtype)


def memattn(p, q_col, g_col, mk, mv, layer, k_col, v_col, b, t, heads, tq=512, batch_block=4):
    n_mem = mk.shape[2]
    w = D_XATTN
    hd = w // heads
    tq = min(tq, t)
    bb = batch_block if tq == t and b % batch_block == 0 else 1
    assert q_col % w == 0 and g_col % w == 0 and k_col % w == 0 and v_col % w == 0
    q_off, g_off, k_off, v_off = q_col // w, g_col // w, k_col // w, v_col // w
    p3 = p.reshape(b, t, p.shape[1])
    out = pl.pallas_call(
        functools.partial(_memattn_kernel, heads=heads, hd=hd, bb=bb),
        grid=(b // bb, t // tq),
        in_specs=[pl.BlockSpec((bb, tq, w), lambda i, j: (i, j, q_off)),
                  pl.BlockSpec((bb, tq, w), lambda i, j: (i, j, g_off)),
                  pl.BlockSpec((None, bb, n_mem, w), lambda i, j: (layer, i, 0, k_off)),
                  pl.BlockSpec((None, bb, n_mem, w), lambda i, j: (layer, i, 0, v_off))],
        out_specs=pl.BlockSpec((bb, tq, w), lambda i, j: (i, j, 0)),
        out_shape=jax.ShapeDtypeStruct((b, t, w), BF16),
        compiler_params=_cparams("parallel", "parallel"),
        name="memattn",
    )(p3, p3, mk, mv)
    return out.reshape(b * t, w)


def _hgrn_kernel(q_ref, f_ref, i_ref, gate_ref, loglb_ref, l1m_ref, oml_ref, g_ref, s0_ref,
                 o_ref, sout_ref, st_ref, *, L, Hb, nC):
    hd = HGRN_HEAD_DIM
    tstep = pl.program_id(2)

    @pl.when(tstep == 0)
    def _():
        for hh in range(Hb):
            st_ref[hh] = s0_ref[0, hh].T

    sub = 8
    row = lax.broadcasted_iota(jnp.int32, (sub, hd), 0)
    lane = lax.broadcasted_iota(jnp.int32, (sub, hd), 1)

    def gates(c):
        r0 = pl.multiple_of(c * L, L)
        staged = []
        for hh in range(Hb):
            cs = slice(hh * hd, (hh + 1) * hd)
            q = q_ref[0, pl.ds(r0, L), cs]
            f = f_ref[0, pl.ds(r0, L), cs]
            vh = i_ref[0, pl.ds(r0, L), cs]
            qh = q * _sigmoid(q)
            a = loglb_ref[:, cs]
            cc = l1m_ref[:, cs] + _log_sigmoid(f)
            lf = jnp.maximum(a, cc) + jnp.log1p(jnp.exp(-jnp.abs(a - cc)))
            kh = oml_ref[:, cs] * _sigmoid(-f)
            b = _cumsum_rows(lf, L) * LOG2E
            pieces = []
            for r0p in range(0, L, sub):
                bp, qp = b[r0p:r0p + sub, :], qh[r0p:r0p + sub, :]
                attn = jnp.zeros((sub, hd), F32)
                for s in range(min(L, r0p + sub)):
                    dec = jnp.exp2(bp - b[s:s + 1, :])
                    if s > r0p:
                        dec = jnp.where(row >= s - r0p, dec, 0.0)
                    col = jnp.sum((qp * kh[s:s + 1, :]) * dec, axis=-1, keepdims=True)
                    attn = jnp.where(lane == s, col, attn)
                pieces.append(attn)
            attn = pieces[0] if len(pieces) == 1 else jnp.concatenate(pieces, axis=0)
            b_last = b[L - 1:L, :]
            staged.append(((qh * jnp.exp2(b)).astype(BF16), attn[:, :L].astype(BF16), vh.astype(BF16),
                           (kh * jnp.exp2(b_last - b)).astype(BF16), jnp.exp2(b_last)))
        return tuple(staged)

    def update(c, staged):
        r0 = pl.multiple_of(c * L, L)
        for hh in range(Hb):
            cs = slice(hh * hd, (hh + 1) * hd)
            qe, attn, vh, kd, e_last = staged[hh]
            st = st_ref[hh]
            o = pl.dot(qe, st.astype(BF16), trans_b=True) + pl.dot(attn, vh)
            o = o * lax.rsqrt(jnp.mean(o * o, axis=-1, keepdims=True) + NORM_EPS)
            o_ref[0, pl.ds(r0, L), cs] = ((o * g_ref[:, cs]) * _silu(gate_ref[0, pl.ds(r0, L), cs])).astype(o_ref.dtype)
            st_ref[hh] = st * e_last + pl.dot(vh, kd, trans_a=True)

    def body(c, staged):
        nxt = gates(c + 1)
        update(c, staged)
        return nxt

    update(nC - 1, lax.fori_loop(0, nC - 1, body, gates(0)))

    @pl.when(tstep == pl.num_programs(2) - 1)
    def _():
        for hh in range(Hb):
            sout_ref[0, hh] = st_ref[hh].T


def hgrn_mix(p, gate_col, b, t, lb, norm_g, s_all, layer, hb):
    heads, hd = s_all.shape[2], s_all.shape[3]
    dm = heads * hd
    L = math.gcd(t, HGRN_CHUNK)
    tb = min(t, SEQ_BLOCK)
    w = hb * hd
    nhb = heads // hb
    assert gate_col % w == 0
    p3 = p.reshape(b, t, p.shape[1])
    row = lambda a: a.reshape(1, dm).astype(F32)
    par_spec = pl.BlockSpec((1, w), lambda i, j, tt: (0, j))
    mix, s_out = pl.pallas_call(
        functools.partial(_hgrn_kernel, L=L, Hb=hb, nC=tb // L),
        grid=(b, nhb, t // tb),
        in_specs=[pl.BlockSpec((1, tb, w), lambda i, j, tt: (i, tt, j)),
                  pl.BlockSpec((1, tb, w), lambda i, j, tt: (i, tt, nhb + j)),
                  pl.BlockSpec((1, tb, w), lambda i, j, tt: (i, tt, 2 * nhb + j)),
                  pl.BlockSpec((1, tb, w), lambda i, j, tt: (i, tt, gate_col // w + j)),
                  par_spec, par_spec, par_spec, par_spec,
                  pl.BlockSpec((None, 1, hb, hd, hd), lambda i, j, tt: (layer, i, j, 0, 0))],
        out_specs=[pl.BlockSpec((1, tb, w), lambda i, j, tt: (i, tt, j)),
                   pl.BlockSpec((1, hb, hd, hd), lambda i, j, tt: (i, j, 0, 0))],
        out_shape=[jax.ShapeDtypeStruct((b, t, dm), BF16), jax.ShapeDtypeStruct(s_all.shape[1:], F32)],
        scratch_shapes=[pltpu.VMEM((hb, hd, hd), F32)],
        compiler_params=_cparams("parallel", "parallel", "arbitrary"),
        name="hgrn_mix",
    )(p3, p3, p3, p3, row(jnp.log(lb)), row(jnp.log1p(-lb)), row(1.0 - lb), row(norm_g), s_all)
    return mix.reshape(b * t, dm), s_out


def _mlstm_kernel(bg_ref, q_ref, k_ref, v_ref, op_ref, gate_ref, ig_ref, fg_ref, g_ref, c0_ref, n0_ref, m0_ref,
                  o_ref, cout_ref, nout_ref, mout_ref, c_sc, n_sc, m_sc, *, L, nC, Hb, heads, dqk, dv):
    dqp = MLSTM_DQK_PAD
    tstep = pl.program_id(2)
    h0 = pl.program_id(1) * Hb
    hs = range(Hb)

    @pl.when(tstep == 0)
    def _():
        c_sc[...] = jnp.zeros_like(c_sc)
        n_sc[...] = jnp.zeros_like(n_sc)
        for hh in hs:
            c_sc[hh, 0:dqk, :] = c0_ref[0, hh]
            n_sc[hh, :, 0:dqk] = n0_ref[0, hh]
            m_sc[hh] = m0_ref[0, hh]

    ti = lax.broadcasted_iota(jnp.int32, (L, L), 0)
    si = lax.broadcasted_iota(jnp.int32, (L, L), 1)
    eye, low = ti == si, si <= ti
    col = lambda r: jnp.sum(jnp.where(eye, r, 0.0), axis=1, keepdims=True)
    kscale = dqk ** -0.5

    def chunk(c, carry):
        r0 = pl.multiple_of(c * L, L)
        q = [q_ref[0, pl.ds(r0, L), hh * dqp:(hh + 1) * dqp] for hh in hs]
        k = [k_ref[0, pl.ds(r0, L), hh * dqp:(hh + 1) * dqp] * kscale for hh in hs]
        v = [v_ref[0, pl.ds(r0, L), hh * dv:(hh + 1) * dv] for hh in hs]
        qb = [x.astype(BF16) for x in q]
        vb = [x.astype(BF16) for x in v]
        s_qk = [pl.dot(qb[hh], k[hh].astype(BF16), trans_b=True) for hh in hs]
        q_c = [pl.dot(qb[hh], c_sc[hh].astype(BF16)) for hh in hs]
        w_inter, a, m_t, b_col, ic_col, m_prev = [], [], [], [], [], []
        for hh in hs:
            ic_row = ig_ref[0, hh, pl.ds(c, 1), :] + bg_ref[h0 + hh]
            lf_row = _log_sigmoid(fg_ref[0, hh, pl.ds(c, 1), :] + bg_ref[heads + h0 + hh])
            bc = jnp.sum(jnp.where(low, lf_row, 0.0), axis=1, keepdims=True)
            b_row = jnp.sum(jnp.where(ti <= si, col(lf_row), 0.0), axis=0, keepdims=True)
            log_w = jnp.where(low, bc - b_row + ic_row, -jnp.inf)
            m = m_sc[hh]
            log_inter = bc + m
            mt = jnp.maximum(log_inter, jnp.max(log_w, axis=1, keepdims=True))
            w_inter.append(jnp.exp(log_inter - mt))
            a.append(jnp.exp(log_w - mt) * s_qk[hh])
            m_t.append(mt)
            b_col.append(bc)
            ic_col.append(col(ic_row))
            m_prev.append(m)
        a_v = [pl.dot(a[hh].astype(BF16), vb[hh]) for hh in hs]
        kw, decay = [], []
        for hh in hs:
            m_last = m_t[hh][L - 1:L, :]
            b_last = b_col[hh][L - 1:L, :]
            kw.append(jnp.exp(b_last - b_col[hh] + ic_col[hh] - m_last) * k[hh])
            decay.append(jnp.exp(b_last + m_prev[hh] - m_last))
            m_sc[hh] = m_last
        kw_v = [pl.dot(kw[hh].astype(BF16), vb[hh], trans_a=True) for hh in hs]
        for hh in hs:
            n = n_sc[hh]
            num = w_inter[hh] * q_c[hh] + a_v[hh]
            den = w_inter[hh] * jnp.sum(q[hh] * n, axis=1, keepdims=True) + jnp.sum(a[hh], axis=1, keepdims=True)
            hid = num / jnp.maximum(jnp.abs(den), jnp.exp(-m_t[hh]))
            c_sc[hh] = decay[hh] * c_sc[hh] + kw_v[hh]
            n_sc[hh] = decay[hh] * n + jnp.sum(kw[hh], axis=0, keepdims=True)
            vs = slice(hh * dv, (hh + 1) * dv)
            hn = hid * lax.rsqrt(jnp.mean(hid * hid, axis=1, keepdims=True) + NORM_EPS) * g_ref[:, vs]
            out = (_sigmoid(op_ref[0, pl.ds(r0, L), vs]) * hn) * _silu(gate_ref[0, pl.ds(r0, L), vs])
            o_ref[0, pl.ds(r0, L), vs] = out.astype(o_ref.dtype)
        return carry

    lax.fori_loop(0, nC, chunk, 0)

    @pl.when(tstep == pl.num_programs(2) - 1)
    def _():
        for hh in hs:
            cout_ref[0, hh] = c_sc[hh, 0:dqk, :]
            nout_ref[0, hh] = n_sc[hh, :, 0:dqk]
            mout_ref[0, hh] = m_sc[hh]


def mlstm_mix(p, gates, b, t, b_gate, norm_g, c_all, n_all, m_all, layer, hb=4):
    heads, dqk, dv = c_all.shape[2], c_all.shape[3], c_all.shape[4]
    dqp = MLSTM_DQK_PAD
    dm = heads * dv
    L = math.gcd(t, MLSTM_CHUNK)
    tb = min(t, SEQ_BLOCK)
    nC, nc_step = t // L, tb // L
    nhb = heads // hb
    p3 = p.reshape(b, t, p.shape[1])
    g3 = gates.reshape(b, t, 2 * heads)
    ig = jnp.swapaxes(g3[..., :heads], 1, 2).reshape(b, heads, nC, L)
    fg = jnp.swapaxes(g3[..., heads:], 1, 2).reshape(b, heads, nC, L)
    q_col = 2 * dm + D_GATE + D_XATTN
    q_off, k_off = q_col // (hb * dqp), (q_col + heads * dqp) // (hb * dqp)
    seq = lambda width, off: pl.BlockSpec((1, tb, hb * width), lambda i, j, tt: (i, tt, off + j))
    gate_spec = pl.BlockSpec((1, hb, nc_step, L), lambda i, j, tt: (i, j, tt, 0))
    c_in = pl.BlockSpec((None, 1, hb, dqk, dv), lambda i, j, tt: (layer, i, j, 0, 0))
    n_in = pl.BlockSpec((None, 1, hb, 1, dqk), lambda i, j, tt: (layer, i, j, 0, 0))
    m_in = pl.BlockSpec((None, 1, hb, 1, 1), lambda i, j, tt: (layer, i, j, 0, 0))
    c_out = pl.BlockSpec((1, hb, dqk, dv), lambda i, j, tt: (i, j, 0, 0))
    n_out = pl.BlockSpec((1, hb, 1, dqk), lambda i, j, tt: (i, j, 0, 0))
    m_out = pl.BlockSpec((1, hb, 1, 1), lambda i, j, tt: (i, j, 0, 0))
    mix, c, n, m = pl.pallas_call(
        functools.partial(_mlstm_kernel, L=L, nC=nc_step, Hb=hb, heads=heads, dqk=dqk, dv=dv),
        grid=(b, nhb, t // tb),
        in_specs=[pl.BlockSpec(memory_space=pltpu.SMEM),
                  seq(dqp, q_off), seq(dqp, k_off), seq(dv, 0), seq(dv, nhb), seq(dv, 2 * nhb),
                  gate_spec, gate_spec,
                  pl.BlockSpec((1, hb * dv), lambda i, j, tt: (0, j)),
                  c_in, n_in, m_in],
        out_specs=[seq(dv, 0), c_out, n_out, m_out],
        out_shape=[jax.ShapeDtypeStruct((b, t, dm), BF16),
                   jax.ShapeDtypeStruct((b, heads, dqk, dv), F32),
                   jax.ShapeDtypeStruct((b, heads, 1, dqk), F32),
                   jax.ShapeDtypeStruct((b, heads, 1, 1), F32)],
        scratch_shapes=[pltpu.VMEM((hb, dqp, dv), F32), pltpu.VMEM((hb, 1, dqp), F32), pltpu.VMEM((hb, 1, 1), F32)],
        compiler_params=_cparams("parallel", "parallel", "arbitrary"),
        name="mlstm_mix",
    )(b_gate.astype(F32), p3, p3, p3, p3, p3, ig, fg, norm_g.reshape(1, dm).astype(F32),
      c_all, n_all.reshape(n_all.shape[:3] + (1, dqk)), m_all.reshape(m_all.shape[:3] + (1, 1)))
    return mix.reshape(b * t, dm), c[None], n.reshape(1, b, heads, dqk), m.reshape(1, b, heads)


def _seg_sum(x, lane_lo):
    s0 = jnp.sum(jnp.where(lane_lo, x, 0.0), axis=-1, keepdims=True)
    s1 = jnp.sum(jnp.where(lane_lo, 0.0, x), axis=-1, keepdims=True)
    return jnp.where(lane_lo, s0, s1)


def _rwkv_prep_kernel(k_ref, wlo_ref, alo_ref, w2_ref, a2_ref, w0_ref, a0_ref, kk_ref, ka_ref,
                      ld_ref, lc_ref, km_ref, av_ref, bv_ref, *, L):
    k = k_ref[...]
    lw = w0_ref[...] + _mm(jnp.tanh(wlo_ref[...]), w2_ref[...])
    log_w = -(jnp.maximum(-lw, 0.0) + jnp.log1p(jnp.exp(-jnp.abs(lw)))) - 0.5
    ld = -jnp.exp(log_w)
    ld_ref[...] = ld
    lc_ref[...] = _cumsum_rows(ld, L)
    a = _sigmoid(a0_ref[...] + _mm(alo_ref[...], a2_ref[...]))
    kk = k * kk_ref[...]
    tn = k.shape[1]
    lane_lo = lax.broadcasted_iota(jnp.int32, (k.shape[0], LANES), 1) < RWKV_HEAD_DIM
    for c in range(tn // LANES):
        cs = slice(c * LANES, (c + 1) * LANES)
        kc = kk[:, cs]
        kc = kc / jnp.maximum(jnp.sqrt(_seg_sum(kc * kc, lane_lo)), 1e-12)
        av_ref[:, cs] = -kc
        bv_ref[:, cs] = kc * a[:, cs]
    km_ref[...] = k * (1.0 + (a - 1.0) * ka_ref[...])


def rwkv_prep(p, k_col, lora, w2, a2, w0, a0, k_k, k_a, L, tm=512, tn=512):
    m = p.shape[0]
    dm = w2.shape[1]
    lp = RWKV_LORA_PAD
    tm = min(tm, m)
    assert tm % L == 0
    k_off = k_col // tn
    par = pl.BlockSpec((1, tn), lambda i, j: (0, j))
    out = pl.BlockSpec((tm, tn), lambda i, j: (i, j))
    row = lambda a: a.reshape(1, dm).astype(F32)
    return pl.pallas_call(
        functools.partial(_rwkv_prep_kernel, L=L),
        grid=(m // tm, dm // tn),
        in_specs=[pl.BlockSpec((tm, tn), lambda i, j: (i, k_off + j)),
                  pl.BlockSpec((tm, lp), lambda i, j: (i, 0)),
                  pl.BlockSpec((tm, lp), lambda i, j: (i, 1)),
                  pl.BlockSpec((lp, tn), lambda i, j: (0, j)),
                  pl.BlockSpec((lp, tn), lambda i, j: (0, j)),
                  par, par, par, par],
        out_specs=[out] * 5,
        out_shape=[jax.ShapeDtypeStruct((m, dm), F32)] * 5,
        compiler_params=_cparams("parallel", "parallel"),
        name="rwkv_prep",
    )(p, lora, lora, w2, a2, row(w0), row(a0), row(k_k), row(k_a))


def _rwkv_kernel(r_ref, ld_ref, lc_ref, k_ref, v_ref, a_ref, b_ref, gate_ref, rk_ref, lg_ref, lb_ref, s0_ref,
                 y_ref, sout_ref, sp_ref, *, L, Gp, Ns, nC):
    hd = RWKV_HEAD_DIM
    G = 2 * Gp
    R = G * L
    W = Gp * LANES
    tstep = pl.program_id(2)
    bm = (lax.broadcasted_iota(jnp.int32, (LANES, LANES), 0) // hd
          == lax.broadcasted_iota(jnp.int32, (LANES, LANES), 1) // hd)

    @pl.when(tstep == 0)
    def _():
        for p in range(Ns * Gp):
            x = s0_ref[0, p]
            sp_ref[p] = jnp.where(bm, jnp.concatenate([x, x], axis=1), 0.0)

    head_mask = (lax.broadcasted_iota(jnp.int32, (R, W), 0) // L
                 == lax.broadcasted_iota(jnp.int32, (R, W), 1) // hd)
    ri = lax.broadcasted_iota(jnp.int32, (R, R), 0)
    ci = lax.broadcasted_iota(jnp.int32, (R, R), 1)
    same = (ri // L) == (ci // L)
    strict = same & (ci < ri)
    ri2 = lax.broadcasted_iota(jnp.int32, (R, 2 * R), 0)
    ci2 = lax.broadcasted_iota(jnp.int32, (R, 2 * R), 1)
    ci2 = jnp.where(ci2 >= R, ci2 - R, ci2)
    incl2 = ((ri2 // L) == (ci2 // L)) & (ci2 <= ri2)
    lo = (lax.broadcasted_iota(jnp.int32, (L, LANES), 1) < hd)
    n_dbl = max(1, (L - 1).bit_length())
    stack = lambda x: jnp.where(head_mask, jnp.concatenate([x] * G, axis=0), 0.0).astype(BF16)

    streams = range(Ns)

    def chunk(c, carry):
        r0 = pl.multiple_of(c * L, L)
        ld_ = lambda ref, s: ref[0, pl.ds(r0, L), s * W:(s + 1) * W]
        ar2, bk2, v2, bkc, c_last = [], [], [], [], []
        for s in streams:
            r, ldec, lc, k, v, av, bv = (ld_(r_ref, s), ld_(ld_ref, s), ld_(lc_ref, s), ld_(k_ref, s),
                                         ld_(v_ref, s), ld_(a_ref, s), ld_(b_ref, s))
            lcl = lc[L - 1:L, :]
            e_neg = jnp.exp(-lc)
            e_rem = jnp.exp(lcl - lc)
            ar2.append(jnp.concatenate([stack(av * jnp.exp(lc - ldec)), stack(r * jnp.exp(lc))], axis=0))
            bk2.append(jnp.concatenate([stack(bv * e_neg), stack(k * e_neg)], axis=0))
            v2.append(stack(v))
            bkc.append(jnp.concatenate([stack(bv * e_rem), stack(k * e_rem)], axis=0))
            c_last.append(jnp.exp(lcl))
        g = [pl.dot(ar2[s], bk2[s], trans_b=True) for s in streams]
        nmat = [jnp.where(strict, g[s][:R, :R], 0.0).astype(BF16) for s in streams]
        a_ak = [jnp.where(strict, g[s][:R, R:], 0.0).astype(BF16) for s in streams]
        a_rbk = [jnp.where(incl2, g[s][R:, :], 0.0).astype(BF16) for s in streams]
        akv = [pl.dot(a_ak[s], v2[s]) for s in streams]
        minv = [nmat[s].astype(F32) for s in streams]
        for it in range(1, n_dbl):
            nmat = [pl.dot(nmat[s], nmat[s]).astype(BF16) for s in streams]
            minv = [minv[s] + nmat[s] + pl.dot(nmat[s], minv[s].astype(BF16)) for s in streams]
        ur0 = [jnp.concatenate(
            [pl.dot(ar2[s][:, p * LANES:(p + 1) * LANES], sp_ref[s * Gp + p].astype(BF16), trans_b=True)
             for p in range(Gp)], axis=1) for s in streams]
        x = [ur0[s][:R] + akv[s] for s in streams]
        x = [x[s] + pl.dot(minv[s].astype(BF16), x[s].astype(BF16)) for s in streams]
        xv = [jnp.concatenate([x[s].astype(BF16), v2[s]], axis=0) for s in streams]
        y2 = [ur0[s][R:] + pl.dot(a_rbk[s], xv[s]) for s in streams]
        for s in streams:
            y = y2[s][0:L]
            for j in range(1, G):
                y = y + y2[s][j * L:(j + 1) * L]
            for p in range(Gp):
                cs = slice(p * LANES, (p + 1) * LANES)
                gs = slice(s * W + p * LANES, s * W + (p + 1) * LANES)
                q = s * Gp + p
                sp_ref[q] = sp_ref[q] * c_last[s][:, cs] + pl.dot(xv[s][:, cs], bkc[s][:, cs], trans_a=True)
                yp = y[:, cs]
                rp, kp, vp = r_ref[0, pl.ds(r0, L), gs], k_ref[0, pl.ds(r0, L), gs], v_ref[0, pl.ds(r0, L), gs]
                mean = _seg_sum(yp, lo) * (1.0 / hd)
                yc = yp - mean
                var = _seg_sum(yc * yc, lo) * (1.0 / hd)
                yn = yc * lax.rsqrt(var + RWKV_GN_EPS) * lg_ref[:, gs] + lb_ref[:, gs]
                bonus = _seg_sum(rp * kp * rk_ref[:, gs], lo) * vp
                y_ref[0, pl.ds(r0, L), gs] = ((yn + bonus) * _silu(gate_ref[0, pl.ds(r0, L), gs])).astype(y_ref.dtype)
        return carry

    lax.fori_loop(0, nC, chunk, 0)

    @pl.when(tstep == pl.num_programs(2) - 1)
    def _():
        row_lo = lax.broadcasted_iota(jnp.int32, (LANES, hd), 0) < hd
        for p in range(Ns * Gp):
            sp = sp_ref[p]
            sout_ref[0, p] = jnp.where(row_lo, sp[:, :hd], sp[:, hd:])


def rwkv_mix(r_src, r_col, v_col, g_col, ld, lc, km, av, bv, b, t, L, r_k, ln_g, ln_b, s0):
    heads, hd = s0.shape[1], s0.shape[2]
    dm = heads * hd
    pairs = heads // 2
    gp = max(g for g in range(1, pairs + 1)
             if pairs % g == 0 and 2 * g * L <= RWKV_ROWS and (2 * g * L) % LANES == 0)
    ns = max(n for n in range(1, RWKV_STREAMS + 1) if (pairs // gp) % n == 0)
    tb = min(t, RWKV_SEQ_BLOCK)
    w = ns * gp * LANES
    assert r_col % w == 0 and v_col % w == 0 and g_col % w == 0
    r_off, v_off, g_off = r_col // w, v_col // w, g_col // w
    three = lambda a: a.reshape(b, t, a.shape[1])
    row = lambda a: a.reshape(1, dm).astype(F32)
    seq = lambda off: pl.BlockSpec((1, tb, w), lambda i, j, tt: (i, tt, off + j))
    par = pl.BlockSpec((1, w), lambda i, j, tt: (0, j))
    st_spec = pl.BlockSpec((1, ns * gp, LANES, hd), lambda i, j, tt: (i, j, 0, 0))
    s0p = s0.reshape(b, pairs, LANES, hd)
    y, s_out = pl.pallas_call(
        functools.partial(_rwkv_kernel, L=L, Gp=gp, Ns=ns, nC=tb // L),
        grid=(b, pairs // (gp * ns), t // tb),
        in_specs=[seq(r_off), seq(0), seq(0), seq(0), seq(v_off), seq(0), seq(0), seq(g_off), par, par, par, st_spec],
        out_specs=[seq(0), st_spec],
        out_shape=[jax.ShapeDtypeStruct((b, t, dm), BF16), jax.ShapeDtypeStruct(s0p.shape, F32)],
        scratch_shapes=[pltpu.VMEM((ns * gp, LANES, LANES), F32)],
        compiler_params=_cparams("parallel", "parallel", "arbitrary"),
        name="rwkv_mix",
    )(three(r_src), three(ld), three(lc), three(km), three(r_src), three(av), three(bv), three(r_src),
      row(r_k), row(ln_g), row(ln_b), s0p)
    return y.reshape(b * t, dm), s_out.reshape(s0.shape)


def _hgrn_lower_bounds(logits):
    p = jax.nn.softmax(logits.astype(F32), axis=0)
    return jnp.maximum(jnp.cumsum(p, axis=0) - p[0], 0.0)


def _mlstm_weights(w_in):
    dq = MLSTM_HEADS * MLSTM_DQK
    o = [0, dq, 2 * dq, 2 * dq + D_MIX, 2 * dq + D_MIX + MLSTM_HEADS, 2 * dq + D_MIX + 2 * MLSTM_HEADS]
    q, k, v = w_in[:, o[0]:o[1]], w_in[:, o[1]:o[2]], w_in[:, o[2]:o[3]]
    gates = w_in[:, o[3]:o[5]]
    rest = w_in[:, o[5]:]
    pad = lambda a: jnp.pad(a.reshape(-1, MLSTM_HEADS, MLSTM_DQK),
                            ((0, 0), (0, 0), (0, MLSTM_DQK_PAD - MLSTM_DQK))).reshape(-1, MLSTM_HEADS * MLSTM_DQK_PAD)
    main = jnp.concatenate([v, rest, pad(q), pad(k)], axis=1).astype(BF16)
    gates = jnp.pad(gates, ((0, 0), (0, LANES - 2 * MLSTM_HEADS))).astype(BF16)
    return main, gates


def _rwkv_weights(w_in, mu):
    c = [0]
    for n in (D_MIX, RWKV_LORA, D_MIX, D_MIX, RWKV_LORA, D_GATE, D_XATTN):
        c.append(c[-1] + n)
    sl = lambda i: w_in[:, c[i]:c[i + 1]]
    main = jnp.concatenate([sl(0), sl(2), sl(3), sl(5), sl(6)], axis=1).astype(BF16)
    padl = lambda a: jnp.pad(a, ((0, 0), (0, RWKV_LORA_PAD - RWKV_LORA)))
    lora = jnp.concatenate([padl(sl(1)), padl(sl(4))], axis=1).astype(BF16)
    mu_ext = jnp.concatenate([mu.astype(F32), jnp.zeros((1, D_MODEL), F32)], axis=0)
    tile = 1024
    main_groups = ([0] * (D_MIX // tile) + [2] * (D_MIX // tile) + [3] * (D_MIX // tile)
                   + [5] * (D_GATE // tile) + [6] * (D_XATTN // tile))
    return main, lora, mu_ext, main_groups, [1, 4]


def _trunk(x3, mk, mv, mem_cols, s_hgrn, s_mc, s_mn, s_mm, s_rwkv, s_shift, prm, hgrn_hb):
    b, t, d = x3.shape
    x = x3.reshape(b * t, d)
    out = dict(hgrn=[], mc=[], mn=[], mm=[], rwkv=[], shift=[])
    for i in range(DEPTH):
        kind, j = i % N_MIXERS, i // N_MIXERS
        if kind == 0:
            h = rmsnorm(x, prm['norm_pre'][i], BF16)
            p = matmul(h, prm['hgrn_w'], layer=j)
            gate_col, q_col = 3 * D_MIX, 3 * D_MIX + D_GATE
            mix, s = hgrn_mix(p, gate_col, b, t, prm['hgrn_lb'][j], prm['hgrn_norm'][j], s_hgrn, j, hgrn_hb)
            out['hgrn'].append(s)
        elif kind == 1:
            h = rmsnorm(x, prm['norm_pre'][i], BF16)
            w_main, w_gates = prm['mlstm_w'][j]
            p = matmul(h, w_main)
            gates = matmul(h, w_gates, tn=LANES)[:, :2 * MLSTM_HEADS]
            gate_col, q_col = 2 * D_MIX, 2 * D_MIX + D_GATE
            mix, c, n, m = mlstm_mix(p, gates, b, t, prm['mlstm_b_gate'][j], prm['mlstm_norm'][j],
                                     s_mc, s_mn, s_mm, j)
            out['mc'].append(c)
            out['mn'].append(n)
            out['mm'].append(m)
        else:
            h = rmsnorm(x, prm['norm_pre'][i], F32)
            h3 = h.reshape(b, t, d)
            hp = jnp.concatenate([s_shift[j][:, None, :], h3[:, :-1]], axis=1).reshape(b * t, d)
            w_main, w_lora, mu_ext, main_groups, lora_groups = prm['rwkv_w'][j]
            mixes = token_shift_lerp(h, hp, mu_ext)
            p = matmul_grouped(mixes, w_main, main_groups)
            lora = matmul_grouped(mixes, w_lora, lora_groups, tn=RWKV_LORA_PAD)
            L = math.gcd(t, RWKV_CHUNK)
            ld, lc, km, av, bv = rwkv_prep(p, D_MIX, lora, prm['rwkv_w2'][j], prm['rwkv_a2'][j], prm['rwkv_w0'][j],
                                           prm['rwkv_a0'][j], prm['rwkv_k_k'][j], prm['rwkv_k_a'][j], L)
            gate_col, q_col = 3 * D_MIX, 3 * D_MIX + D_GATE
            mix, s = rwkv_mix(p, 0, 2 * D_MIX, gate_col, ld, lc, km, av, bv, b, t, L, prm['rwkv_r_k'][j],
                              prm['rwkv_ln_g'][j], prm['rwkv_ln_b'][j], s_rwkv[j])
            out['rwkv'].append(s)
            out['shift'].append(h3[:, -1])
        lyr, k_col, v_col = mem_cols(i)
        xo = memattn(p, q_col, gate_col + D_MIX, mk[i] if isinstance(mk, list) else mk,
                     mv[i] if isinstance(mv, list) else mv, lyr, k_col, v_col, b, t, X_HEADS)
        x = outproj(mix, xo, prm['w_out'], i, x, prm['norm_post'][i])
    cat = lambda parts: parts[0] if len(parts) == 1 else jnp.concatenate(parts, axis=0)
    return (x.reshape(b, t, d), jnp.stack(out['hgrn']), cat(out['mc']), cat(out['mn']), cat(out['mm']),
            jnp.stack(out['rwkv']), jnp.stack(out['shift']))


def kernel(x_prompt, x_sample, mem_prompt, cache_mem_k, cache_mem_v, state_hgrn, state_mlstm_c, state_mlstm_n, state_mlstm_m, state_rwkv, state_rwkv_shift, norm_pre, norm_post, norm_mem, w_mem_kv, w_out, hgrn_w_in, hgrn_lb_logits, hgrn_norm, mlstm_w_in, mlstm_b_gate, mlstm_norm, rwkv_w_in, rwkv_mu, rwkv_w0, rwkv_w2, rwkv_a0, rwkv_a2, rwkv_k_k, rwkv_k_a, rwkv_r_k, rwkv_ln_g, rwkv_ln_b):
    n_h, n_m, n_r = hgrn_w_in.shape[0], mlstm_w_in.shape[0], rwkv_w_in.shape[0]
    pad_lora = lambda a: jnp.pad(a, ((0, RWKV_LORA_PAD - RWKV_LORA), (0, 0))).astype(BF16)
    prm = dict(
        norm_pre=norm_pre, norm_post=norm_post, w_out=w_out.astype(BF16),
        hgrn_w=hgrn_w_in.astype(BF16),
        hgrn_lb=_hgrn_lower_bounds(hgrn_lb_logits), hgrn_norm=hgrn_norm,
        mlstm_w=[_mlstm_weights(mlstm_w_in[j]) for j in range(n_m)],
        mlstm_b_gate=mlstm_b_gate, mlstm_norm=mlstm_norm,
        rwkv_w=[_rwkv_weights(rwkv_w_in[j], rwkv_mu[j]) for j in range(n_r)],
        rwkv_w0=rwkv_w0, rwkv_w2=[pad_lora(rwkv_w2[j]) for j in range(n_r)],
        rwkv_a0=rwkv_a0, rwkv_a2=[pad_lora(rwkv_a2[j]) for j in range(n_r)],
        rwkv_k_k=rwkv_k_k, rwkv_k_a=rwkv_k_a, rwkv_r_k=rwkv_r_k, rwkv_ln_g=rwkv_ln_g, rwkv_ln_b=rwkv_ln_b)

    bp, n_mem, d = mem_prompt.shape
    mem2 = mem_prompt.reshape(bp * n_mem, d)
    kvs = [matmul(rmsnorm(mem2, norm_mem[i], BF16), w_mem_kv[i].astype(BF16)).reshape(1, bp, n_mem, 2 * D_XATTN)
           for i in range(DEPTH)]
    heads_shape = (bp, n_mem, X_HEADS, X_HEAD_DIM)
    mem_k_prompt = jnp.stack([kv[0, :, :, :D_XATTN].reshape(heads_shape) for kv in kvs])
    mem_v_prompt = jnp.stack([kv[0, :, :, D_XATTN:].reshape(heads_shape) for kv in kvs])

    dt = x_prompt.dtype
    zeros = lambda *shape: jnp.zeros(shape, dt)
    (y_prompt, hgrn_p, mc_p, mn_p, mm_p, rwkv_p, shift_p) = _trunk(
        x_prompt, kvs, kvs, lambda i: (0, 0, D_XATTN),
        zeros(n_h, bp, HGRN_HEADS, HGRN_HEAD_DIM, HGRN_HEAD_DIM),
        zeros(n_m, bp, MLSTM_HEADS, MLSTM_DQK, MLSTM_DV),
        zeros(n_m, bp, MLSTM_HEADS, MLSTM_DQK),
        zeros(n_m, bp, MLSTM_HEADS),
        zeros(n_r, bp, RWKV_HEADS, RWKV_HEAD_DIM, RWKV_HEAD_DIM),
        zeros(n_r, bp, D_MODEL),
        prm, hgrn_hb=6)
    bs = x_sample.shape[0]
    ck = cache_mem_k.reshape(DEPTH, bs, n_mem, D_XATTN)
    cv = cache_mem_v.reshape(DEPTH, bs, n_mem, D_XATTN)
    (y_sample, hgrn_s, mc_s, mn_s, mm_s, rwkv_s, shift_s) = _trunk(
        x_sample, ck, cv, lambda i: (i, 0, 0), state_hgrn, state_mlstm_c, state_mlstm_n, state_mlstm_m,
        state_rwkv, state_rwkv_shift, prm, hgrn_hb=8)
    return (y_prompt, y_sample, mem_k_prompt, mem_v_prompt,
            hgrn_p, mc_p, mn_p, mm_p, rwkv_p, shift_p,
            hgrn_s, mc_s, mn_s, mm_s, rwkv_s, shift_s)
```

```python
import functools
import math

import jax
import jax.numpy as jnp
from jax import lax
from jax.experimental import pallas as pl
from jax.experimental.pallas import tpu as pltpu

F32 = jnp.float32
BF16 = jnp.bfloat16

D_MODEL = 4096
DEPTH = 4
N_MIXERS = 3
D_MIX = 3 * D_MODEL // 4
D_XATTN = D_MODEL // 4
D_GATE = D_MIX + D_XATTN
N_MEM = 256
X_HEADS = 4
X_HEAD_DIM = D_XATTN // X_HEADS

HGRN_HEAD_DIM = 128
HGRN_HEADS = D_MIX // HGRN_HEAD_DIM
HGRN_CHUNK = 16

MLSTM_HEADS = 8
MLSTM_DV = D_MIX // MLSTM_HEADS
MLSTM_DQK = MLSTM_DV // 2
MLSTM_DQK_PAD = 256
MLSTM_CHUNK = 64

RWKV_HEAD_DIM = 64
RWKV_HEADS = D_MIX // RWKV_HEAD_DIM
RWKV_LORA = max(32, int(round(1.8 * math.sqrt(D_MIX) / 32)) * 32)
RWKV_LORA_PAD = 128
RWKV_GN_EPS = 64e-5
RWKV_CHUNK = 64
RWKV_ROWS = 256
RWKV_STREAMS = 3
RWKV_SEQ_BLOCK = 512

NORM_EPS = 1e-6
LOG2E = 1.4426950408889634
LANES = 128
SEQ_BLOCK = 512
VMEM_LIMIT = 48 * 1024 * 1024
VMEM_LIMIT_OUTPROJ = 56 * 1024 * 1024


def _cparams(*sem, vmem=VMEM_LIMIT):
    return pltpu.CompilerParams(dimension_semantics=sem, vmem_limit_bytes=vmem)


def _mm(a, b, **kw):
    return pl.dot(a.astype(BF16), b.astype(BF16), **kw)


def _sigmoid(x):
    return 1.0 / (1.0 + jnp.exp(-x))


def _silu(g):
    return g * _sigmoid(g)


def _log_sigmoid(x):
    return jnp.minimum(x, 0.0) - jnp.log1p(jnp.exp(-jnp.abs(x)))


def _cumsum_rows(x, block):
    row = lax.broadcasted_iota(jnp.int32, x.shape, 0) & (block - 1)
    sh = 1
    while sh < block:
        x = x + jnp.where(row >= sh, pltpu.roll(x, sh, axis=0), 0.0)
        sh *= 2
    return x


def _rmsnorm_kernel(x_ref, g_ref, o_ref):
    x = x_ref[...]
    ms = jnp.mean(x * x, axis=-1, keepdims=True)
    o_ref[...] = ((x * lax.rsqrt(ms + NORM_EPS)) * g_ref[...]).astype(o_ref.dtype)


def rmsnorm(x, gain, out_dtype, tm=256):
    m, d = x.shape
    tm = min(tm, m)
    return pl.pallas_call(
        _rmsnorm_kernel,
        grid=(m // tm,),
        in_specs=[pl.BlockSpec((tm, d), lambda i: (i, 0)), pl.BlockSpec((1, d), lambda i: (0, 0))],
        out_specs=pl.BlockSpec((tm, d), lambda i: (i, 0)),
        out_shape=jax.ShapeDtypeStruct((m, d), out_dtype),
        compiler_params=_cparams("parallel"),
        name="rmsnorm",
    )(x, gain.reshape(1, d))


def _mm_kernel(a_ref, w_ref, o_ref):
    o_ref[...] = _mm(a_ref[...], w_ref[...])


def matmul(a, w, layer=None, tm=1024, tn=1024):
    m, k = a.shape
    n = w.shape[-1]
    tm, tn = min(tm, m), min(tn, n)
    if layer is None:
        w_spec = pl.BlockSpec((k, tn), lambda i, j: (0, j))
    else:
        w_spec = pl.BlockSpec((None, k, tn), lambda i, j: (layer, 0, j))
    return pl.pallas_call(
        _mm_kernel,
        grid=(m // tm, n // tn),
        in_specs=[pl.BlockSpec((tm, k), lambda i, j: (i, 0)), w_spec],
        out_specs=pl.BlockSpec((tm, tn), lambda i, j: (i, j)),
        out_shape=jax.ShapeDtypeStruct((m, n), F32),
        compiler_params=_cparams("parallel", "parallel"),
        name="matmul",
    )(a, w)


def _lerp_kernel(h_ref, hp_ref, mu_ref, o_ref):
    h = h_ref[...]
    d = hp_ref[...] - h
    for g in range(o_ref.shape[0]):
        o_ref[g] = (h + d * mu_ref[g]).astype(o_ref.dtype)


def token_shift_lerp(h, hp, mu, tm=128):
    m, k = h.shape
    g = mu.shape[0]
    tm = min(tm, m)
    return pl.pallas_call(
        _lerp_kernel,
        grid=(m // tm,),
        in_specs=[pl.BlockSpec((tm, k), lambda i: (i, 0)),
                  pl.BlockSpec((tm, k), lambda i: (i, 0)),
                  pl.BlockSpec((g, 1, k), lambda i: (0, 0, 0))],
        out_specs=pl.BlockSpec((g, tm, k), lambda i: (0, i, 0)),
        out_shape=jax.ShapeDtypeStruct((g, m, k), BF16),
        compiler_params=_cparams("parallel"),
        name="token_shift_lerp",
    )(h, hp, mu.reshape(g, 1, k).astype(F32))


def matmul_grouped(a, w, tile_group, tm=1024, tn=1024):
    _, m, k = a.shape
    n = w.shape[1]
    tm, tn = min(tm, m), min(tn, n)
    assert len(tile_group) == n // tn and list(tile_group) == sorted(tile_group)

    def group(j):
        g = tile_group[0]
        for t in range(1, len(tile_group)):
            if tile_group[t] != tile_group[t - 1]:
                g = g + jnp.where(j >= t, tile_group[t] - tile_group[t - 1], 0)
        return g

    return pl.pallas_call(
        _mm_kernel,
        grid=(m // tm, n // tn),
        in_specs=[pl.BlockSpec((None, tm, k), lambda i, j: (group(j), i, 0)),
                  pl.BlockSpec((k, tn), lambda i, j: (0, j))],
        out_specs=pl.BlockSpec((tm, tn), lambda i, j: (i, j)),
        out_shape=jax.ShapeDtypeStruct((m, n), F32),
        compiler_params=_cparams("parallel", "parallel"),
        name="matmul_grouped",
    )(a, w)


def _outproj_kernel(mix_ref, xo_ref, w_ref, x_ref, gain_ref, o_ref):
    d_mix = mix_ref.shape[1]
    y = pl.dot(mix_ref[...], w_ref[0:d_mix, :]) + pl.dot(xo_ref[...], w_ref[d_mix:, :])
    ms = jnp.mean(y * y, axis=-1, keepdims=True)
    o_ref[...] = x_ref[...] + (y * lax.rsqrt(ms + NORM_EPS)) * gain_ref[...]


def outproj(u_mix, u_xo, w_all, layer, x, gain, tm=128):
    m, d_mix = u_mix.shape
    d_x = u_xo.shape[1]
    kdim, n = w_all.shape[1:]
    tm = min(tm, m)
    assert kdim == d_mix + d_x
    return pl.pallas_call(
        _outproj_kernel,
        grid=(m // tm,),
        in_specs=[pl.BlockSpec((tm, d_mix), lambda i: (i, 0)),
                  pl.BlockSpec((tm, d_x), lambda i: (i, 0)),
                  pl.BlockSpec((None, kdim, n), lambda i: (layer, 0, 0), pipeline_mode=pl.Buffered(1)),
                  pl.BlockSpec((tm, n), lambda i: (i, 0)),
                  pl.BlockSpec((1, n), lambda i: (0, 0))],
        out_specs=pl.BlockSpec((tm, n), lambda i: (i, 0)),
        out_shape=jax.ShapeDtypeStruct((m, n), F32),
        compiler_params=_cparams("parallel", vmem=VMEM_LIMIT_OUTPROJ),
        name="outproj",
    )(u_mix, u_xo, w_all, x, gain.reshape(1, n))


def _memattn_kernel(q_ref, g_ref, k_ref, v_ref, o_ref, *, heads, hd, bb):
    scale = hd ** -0.5
    for i in range(bb):
        for h in range(heads):
            cs = slice(h * hd, (h + 1) * hd)
            s = _mm(q_ref[i, :, cs], k_ref[i, :, cs], trans_b=True) * scale
            s = s - jnp.max(s, axis=-1, keepdims=True)
            e = jnp.exp(s)
            p = e / jnp.sum(e, axis=-1, keepdims=True)
            o_ref[i, :, cs] = (_mm(p, v_ref[i, :, cs]) * _silu(g_ref[i, :, cs])).astype(o_ref.dtype)


def memattn(p, q_col, g_col, mk, mv, layer, k_col, v_col, b, t, heads, tq=512, batch_block=4):
    n_mem = mk.shape[2]
    w = D_XATTN
    hd = w // heads
    tq = min(tq, t)
    bb = batch_block if tq == t and b % batch_block == 0 else 1
    assert q_col % w == 0 and g_col % w == 0 and k_col % w == 0 and v_col % w == 0
    q_off, g_off, k_off, v_off = q_col // w, g_col // w, k_col // w, v_col // w
    p3 = p.reshape(b, t, p.shape[1])
    out = pl.pallas_call(
        functools.partial(_memattn_kernel, heads=heads, hd=hd, bb=bb),
        grid=(b // bb, t // tq),
        in_specs=[pl.BlockSpec((bb, tq, w), lambda i, j: (i, j, q_off)),
                  pl.BlockSpec((bb, tq, w), lambda i, j: (i, j, g_off)),
                  pl.BlockSpec((None, bb, n_mem, w), lambda i, j: (layer, i, 0, k_off)),
                  pl.BlockSpec((None, bb, n_mem, w), lambda i, j: (layer, i, 0, v_off))],
        out_specs=pl.BlockSpec((bb, tq, w), lambda i, j: (i, j, 0)),
        out_shape=jax.ShapeDtypeStruct((b, t, w), BF16),
        compiler_params=_cparams("parallel", "parallel"),
        name="memattn",
    )(p3, p3, mk, mv)
    return out.reshape(b * t, w)


def _memattn_cache_kernel(q_ref, g_ref, k_ref, v_ref, o_ref, *, heads, rows, bb):
    n_col = k_ref.shape[1]
    scale = (2 * LANES) ** -0.5
    col = lax.broadcasted_iota(jnp.int32, (rows, n_col), 1)
    same_head = (col % heads) == (lax.broadcasted_iota(jnp.int32, (rows, n_col), 0) % heads)
    half = (col // heads) % 2
    keep = [same_head & (half == 0), same_head & (half == 1)]
    for i in range(bb):
        k = k_ref[i].astype(BF16)
        v = v_ref[i].astype(BF16)
        s = [jnp.where(keep[c], _mm(q_ref[i, c], k, trans_b=True), 0.0) for c in range(2)]
        s = s[0] + s[1]
        s = s + pltpu.roll(s, n_col - heads, axis=1)
        s = jnp.where(keep[0], s * scale, -jnp.inf)
        e = jnp.exp(s - jnp.max(s, axis=-1, keepdims=True))
        p = e / jnp.sum(e, axis=-1, keepdims=True)
        for c in range(2):
            pc = p if c == 0 else pltpu.roll(p, heads, axis=1)
            o_ref[i, c] = (pl.dot(pc.astype(BF16), v) * _silu(g_ref[i, c])).astype(o_ref.dtype)


def memattn_cache(p, q_col, g_col, cache_k, cache_v, layer, b, t, batch_block=4):
    heads = cache_k.shape[3]
    assert cache_k.shape[4] == 2 * LANES
    n_mem = cache_k.shape[2]
    rows, w = t * heads, heads * 2 * LANES
    bb = batch_block if b % batch_block == 0 else 1

    def rows_view(a):
        a = a.reshape(a.shape[0], b, n_mem, heads, 2, LANES).transpose(0, 1, 2, 4, 3, 5)
        return a.reshape(a.shape[0] * b, n_mem * 2 * heads, LANES)

    def halves(a):
        return a.reshape(b, t, heads, 2, LANES).transpose(0, 3, 1, 2, 4).reshape(b, 2, rows, LANES)

    tok = pl.BlockSpec((bb, 2, rows, LANES), lambda i: (i, 0, 0, 0))
    mem = pl.BlockSpec((bb, n_mem * 2 * heads, LANES), lambda i: (layer * (b // bb) + i, 0, 0))
    out = pl.pallas_call(
        functools.partial(_memattn_cache_kernel, heads=heads, rows=rows, bb=bb),
        grid=(b // bb,),
        in_specs=[tok, tok, mem, mem],
        out_specs=tok,
        out_shape=jax.ShapeDtypeStruct((b, 2, rows, LANES), BF16),
        compiler_params=_cparams("parallel"),
        name="memattn_cache",
    )(halves(p[:, q_col:q_col + w]), halves(p[:, g_col:g_col + w]), rows_view(cache_k), rows_view(cache_v))
    return out.reshape(b, 2, t, heads, LANES).transpose(0, 2, 3, 1, 4).reshape(b * t, w)


def _hgrn_kernel(q_ref, f_ref, i_ref, gate_ref, loglb_ref, l1m_ref, oml_ref, g_ref, s0_ref,
                 o_ref, sout_ref, st_ref, *, L, Hb, nC):
    hd = HGRN_HEAD_DIM
    tstep = pl.program_id(2)

    @pl.when(tstep == 0)
    def _():
        for hh in range(Hb):
            st_ref[hh] = s0_ref[0, hh].T

    sub = 8
    row = lax.broadcasted_iota(jnp.int32, (sub, hd), 0)
    lane = lax.broadcasted_iota(jnp.int32, (sub, hd), 1)

    def gates(c):
        r0 = pl.multiple_of(c * L, L)
        staged = []
        for hh in range(Hb):
            cs = slice(hh * hd, (hh + 1) * hd)
            q = q_ref[0, pl.ds(r0, L), cs]
            f = f_ref[0, pl.ds(r0, L), cs]
            vh = i_ref[0, pl.ds(r0, L), cs]
            qh = q * _sigmoid(q)
            a = loglb_ref[:, cs]
            cc = l1m_ref[:, cs] + _log_sigmoid(f)
            lf = jnp.maximum(a, cc) + jnp.log1p(jnp.exp(-jnp.abs(a - cc)))
            kh = oml_ref[:, cs] * _sigmoid(-f)
            b = _cumsum_rows(lf, L) * LOG2E
            pieces = []
            for r0p in range(0, L, sub):
                bp, qp = b[r0p:r0p + sub, :], qh[r0p:r0p + sub, :]
                attn = jnp.zeros((sub, hd), F32)
                for s in range(min(L, r0p + sub)):
                    dec = jnp.exp2(bp - b[s:s + 1, :])
                    if s > r0p:
                        dec = jnp.where(row >= s - r0p, dec, 0.0)
                    col = jnp.sum((qp * kh[s:s + 1, :]) * dec, axis=-1, keepdims=True)
                    attn = jnp.where(lane == s, col, attn)
                pieces.append(attn)
            attn = pieces[0] if len(pieces) == 1 else jnp.concatenate(pieces, axis=0)
            b_last = b[L - 1:L, :]
            staged.append(((qh * jnp.exp2(b)).astype(BF16), attn[:, :L].astype(BF16), vh.astype(BF16),
                           (kh * jnp.exp2(b_last - b)).astype(BF16), jnp.exp2(b_last)))
        return tuple(staged)

    def update(c, staged):
        r0 = pl.multiple_of(c * L, L)
        for hh in range(Hb):
            cs = slice(hh * hd, (hh + 1) * hd)
            qe, attn, vh, kd, e_last = staged[hh]
            st = st_ref[hh]
            o = pl.dot(qe, st.astype(BF16), trans_b=True) + pl.dot(attn, vh)
            o = o * lax.rsqrt(jnp.mean(o * o, axis=-1, keepdims=True) + NORM_EPS)
            o_ref[0, pl.ds(r0, L), cs] = ((o * g_ref[:, cs]) * _silu(gate_ref[0, pl.ds(r0, L), cs])).astype(o_ref.dtype)
            st_ref[hh] = st * e_last + pl.dot(vh, kd, trans_a=True)

    def body(c, staged):
        nxt = gates(c + 1)
        update(c, staged)
        return nxt

    update(nC - 1, lax.fori_loop(0, nC - 1, body, gates(0)))

    @pl.when(tstep == pl.num_programs(2) - 1)
    def _():
        for hh in range(Hb):
            sout_ref[0, hh] = st_ref[hh].T


def hgrn_mix(p, gate_col, b, t, lb, norm_g, s_all, layer, hb):
    heads, hd = s_all.shape[2], s_all.shape[3]
    dm = heads * hd
    L = math.gcd(t, HGRN_CHUNK)
    tb = min(t, SEQ_BLOCK)
    w = hb * hd
    nhb = heads // hb
    assert gate_col % w == 0
    p3 = p.reshape(b, t, p.shape[1])
    row = lambda a: a.reshape(1, dm).astype(F32)
    par_spec = pl.BlockSpec((1, w), lambda i, j, tt: (0, j))
    mix, s_out = pl.pallas_call(
        functools.partial(_hgrn_kernel, L=L, Hb=hb, nC=tb // L),
        grid=(b, nhb, t // tb),
        in_specs=[pl.BlockSpec((1, tb, w), lambda i, j, tt: (i, tt, j)),
                  pl.BlockSpec((1, tb, w), lambda i, j, tt: (i, tt, nhb + j)),
                  pl.BlockSpec((1, tb, w), lambda i, j, tt: (i, tt, 2 * nhb + j)),
                  pl.BlockSpec((1, tb, w), lambda i, j, tt: (i, tt, gate_col // w + j)),
                  par_spec, par_spec, par_spec, par_spec,
                  pl.BlockSpec((None, 1, hb, hd, hd), lambda i, j, tt: (layer, i, j, 0, 0))],
        out_specs=[pl.BlockSpec((1, tb, w), lambda i, j, tt: (i, tt, j)),
                   pl.BlockSpec((1, hb, hd, hd), lambda i, j, tt: (i, j, 0, 0))],
        out_shape=[jax.ShapeDtypeStruct((b, t, dm), BF16), jax.ShapeDtypeStruct(s_all.shape[1:], F32)],
        scratch_shapes=[pltpu.VMEM((hb, hd, hd), F32)],
        compiler_params=_cparams("parallel", "parallel", "arbitrary"),
        name="hgrn_mix",
    )(p3, p3, p3, p3, row(jnp.log(lb)), row(jnp.log1p(-lb)), row(1.0 - lb), row(norm_g), s_all)
    return mix.reshape(b * t, dm), s_out


def _mlstm_kernel(bg_ref, q_ref, k_ref, v_ref, op_ref, gate_ref, ig_ref, fg_ref, g_ref, c0_ref, n0_ref, m0_ref,
                  o_ref, cout_ref, nout_ref, mout_ref, c_sc, n_sc, m_sc, *, L, nC, Hb, heads, dqk, dv):
    dqp = MLSTM_DQK_PAD
    tstep = pl.program_id(2)
    h0 = pl.program_id(1) * Hb
    hs = range(Hb)

    @pl.when(tstep == 0)
    def _():
        c_sc[...] = jnp.zeros_like(c_sc)
        n_sc[...] = jnp.zeros_like(n_sc)
        for hh in hs:
            c_sc[hh, 0:dqk, :] = c0_ref[0, hh]
            n_sc[hh, :, 0:dqk] = n0_ref[0, hh]
            m_sc[hh] = m0_ref[0, hh]

    ti = lax.broadcasted_iota(jnp.int32, (L, L), 0)
    si = lax.broadcasted_iota(jnp.int32, (L, L), 1)
    eye, low = ti == si, si <= ti
    col = lambda r: jnp.sum(jnp.where(eye, r, 0.0), axis=1, keepdims=True)
    kscale = dqk ** -0.5

    def chunk(c, carry):
        r0 = pl.multiple_of(c * L, L)
        q = [q_ref[0, pl.ds(r0, L), hh * dqp:(hh + 1) * dqp] for hh in hs]
        k = [k_ref[0, pl.ds(r0, L), hh * dqp:(hh + 1) * dqp] * kscale for hh in hs]
        v = [v_ref[0, pl.ds(r0, L), hh * dv:(hh + 1) * dv] for hh in hs]
        qb = [x.astype(BF16) for x in q]
        vb = [x.astype(BF16) for x in v]
        s_qk = [pl.dot(qb[hh], k[hh].astype(BF16), trans_b=True) for hh in hs]
        q_c = [pl.dot(qb[hh], c_sc[hh].astype(BF16)) for hh in hs]
        w_inter, a, m_t, b_col, ic_col, m_prev = [], [], [], [], [], []
        for hh in hs:
            ic_row = ig_ref[0, hh, pl.ds(c, 1), :] + bg_ref[h0 + hh]
            lf_row = _log_sigmoid(fg_ref[0, hh, pl.ds(c, 1), :] + bg_ref[heads + h0 + hh])
            bc = jnp.sum(jnp.where(low, lf_row, 0.0), axis=1, keepdims=True)
            b_row = jnp.sum(jnp.where(ti <= si, col(lf_row), 0.0), axis=0, keepdims=True)
            log_w = jnp.where(low, bc - b_row + ic_row, -jnp.inf)
            m = m_sc[hh]
            log_inter = bc + m
            mt = jnp.maximum(log_inter, jnp.max(log_w, axis=1, keepdims=True))
            w_inter.append(jnp.exp(log_inter - mt))
            a.append(jnp.exp(log_w - mt) * s_qk[hh])
            m_t.append(mt)
            b_col.append(bc)
            ic_col.append(col(ic_row))
            m_prev.append(m)
        a_v = [pl.dot(a[hh].astype(BF16), vb[hh]) for hh in hs]
        kw, decay = [], []
        for hh in hs:
            m_last = m_t[hh][L - 1:L, :]
            b_last = b_col[hh][L - 1:L, :]
            kw.append(jnp.exp(b_last - b_col[hh] + ic_col[hh] - m_last) * k[hh])
            decay.append(jnp.exp(b_last + m_prev[hh] - m_last))
            m_sc[hh] = m_last
        kw_v = [pl.dot(kw[hh].astype(BF16), vb[hh], trans_a=True) for hh in hs]
        for hh in hs:
            n = n_sc[hh]
            num = w_inter[hh] * q_c[hh] + a_v[hh]
            den = w_inter[hh] * jnp.sum(q[hh] * n, axis=1, keepdims=True) + jnp.sum(a[hh], axis=1, keepdims=True)
            hid = num / jnp.maximum(jnp.abs(den), jnp.exp(-m_t[hh]))
            c_sc[hh] = decay[hh] * c_sc[hh] + kw_v[hh]
            n_sc[hh] = decay[hh] * n + jnp.sum(kw[hh], axis=0, keepdims=True)
            vs = slice(hh * dv, (hh + 1) * dv)
            hn = hid * lax.rsqrt(jnp.mean(hid * hid, axis=1, keepdims=True) + NORM_EPS) * g_ref[:, vs]
            out = (_sigmoid(op_ref[0, pl.ds(r0, L), vs]) * hn) * _silu(gate_ref[0, pl.ds(r0, L), vs])
            o_ref[0, pl.ds(r0, L), vs] = out.astype(o_ref.dtype)
        return carry

    lax.fori_loop(0, nC, chunk, 0)

    @pl.when(tstep == pl.num_programs(2) - 1)
    def _():
        for hh in hs:
            cout_ref[0, hh] = c_sc[hh, 0:dqk, :]
            nout_ref[0, hh] = n_sc[hh, :, 0:dqk]
            mout_ref[0, hh] = m_sc[hh]


def mlstm_mix(p, gates, b, t, b_gate, norm_g, c_all, n_all, m_all, layer, hb=4):
    heads, dqk, dv = c_all.shape[2], c_all.shape[3], c_all.shape[4]
    dqp = MLSTM_DQK_PAD
    dm = heads * dv
    L = math.gcd(t, MLSTM_CHUNK)
    tb = min(t, SEQ_BLOCK)
    nC, nc_step = t // L, tb // L
    nhb = heads // hb
    p3 = p.reshape(b, t, p.shape[1])
    g3 = gates.reshape(b, t, 2 * heads)
    ig = jnp.swapaxes(g3[..., :heads], 1, 2).reshape(b, heads, nC, L)
    fg = jnp.swapaxes(g3[..., heads:], 1, 2).reshape(b, heads, nC, L)
    q_col = 2 * dm + D_GATE + D_XATTN
    q_off, k_off = q_col // (hb * dqp), (q_col + heads * dqp) // (hb * dqp)
    seq = lambda width, off: pl.BlockSpec((1, tb, hb * width), lambda i, j, tt: (i, tt, off + j))
    gate_spec = pl.BlockSpec((1, hb, nc_step, L), lambda i, j, tt: (i, j, tt, 0))
    c_in = pl.BlockSpec((None, 1, hb, dqk, dv), lambda i, j, tt: (layer, i, j, 0, 0))
    n_in = pl.BlockSpec((None, 1, hb, 1, dqk), lambda i, j, tt: (layer, i, j, 0, 0))
    m_in = pl.BlockSpec((None, 1, hb, 1, 1), lambda i, j, tt: (layer, i, j, 0, 0))
    c_out = pl.BlockSpec((1, hb, dqk, dv), lambda i, j, tt: (i, j, 0, 0))
    n_out = pl.BlockSpec((1, hb, 1, dqk), lambda i, j, tt: (i, j, 0, 0))
    m_out = pl.BlockSpec((1, hb, 1, 1), lambda i, j, tt: (i, j, 0, 0))
    mix, c, n, m = pl.pallas_call(
        functools.partial(_mlstm_kernel, L=L, nC=nc_step, Hb=hb, heads=heads, dqk=dqk, dv=dv),
        grid=(b, nhb, t // tb),
        in_specs=[pl.BlockSpec(memory_space=pltpu.SMEM),
                  seq(dqp, q_off), seq(dqp, k_off), seq(dv, 0), seq(dv, nhb), seq(dv, 2 * nhb),
                  gate_spec, gate_spec,
                  pl.BlockSpec((1, hb * dv), lambda i, j, tt: (0, j)),
                  c_in, n_in, m_in],
        out_specs=[seq(dv, 0), c_out, n_out, m_out],
        out_shape=[jax.ShapeDtypeStruct((b, t, dm), BF16),
                   jax.ShapeDtypeStruct((b, heads, dqk, dv), F32),
                   jax.ShapeDtypeStruct((b, heads, 1, dqk), F32),
                   jax.ShapeDtypeStruct((b, heads, 1, 1), F32)],
        scratch_shapes=[pltpu.VMEM((hb, dqp, dv), F32), pltpu.VMEM((hb, 1, dqp), F32), pltpu.VMEM((hb, 1, 1), F32)],
        compiler_params=_cparams("parallel", "parallel", "arbitrary"),
        name="mlstm_mix",
    )(b_gate.astype(F32), p3, p3, p3, p3, p3, ig, fg, norm_g.reshape(1, dm).astype(F32),
      c_all, n_all.reshape(n_all.shape[:3] + (1, dqk)), m_all.reshape(m_all.shape[:3] + (1, 1)))
    return mix.reshape(b * t, dm), c[None], n.reshape(1, b, heads, dqk), m.reshape(1, b, heads)


def _seg_sum(x, lane_lo):
    s0 = jnp.sum(jnp.where(lane_lo, x, 0.0), axis=-1, keepdims=True)
    s1 = jnp.sum(jnp.where(lane_lo, 0.0, x), axis=-1, keepdims=True)
    return jnp.where(lane_lo, s0, s1)


def _rwkv_prep_kernel(k_ref, wlo_ref, alo_ref, w2_ref, a2_ref, w0_ref, a0_ref, kk_ref, ka_ref,
                      ld_ref, lc_ref, km_ref, av_ref, bv_ref, *, L):
    k = k_ref[...]
    lw = w0_ref[...] + _mm(jnp.tanh(wlo_ref[...]), w2_ref[...])
    log_w = -(jnp.maximum(-lw, 0.0) + jnp.log1p(jnp.exp(-jnp.abs(lw)))) - 0.5
    ld = -jnp.exp(log_w)
    ld_ref[...] = ld
    lc_ref[...] = _cumsum_rows(ld, L)
    a = _sigmoid(a0_ref[...] + _mm(alo_ref[...], a2_ref[...]))
    kk = k * kk_ref[...]
    tn = k.shape[1]
    lane_lo = lax.broadcasted_iota(jnp.int32, (k.shape[0], LANES), 1) < RWKV_HEAD_DIM
    for c in range(tn // LANES):
        cs = slice(c * LANES, (c + 1) * LANES)
        kc = kk[:, cs]
        kc = kc / jnp.maximum(jnp.sqrt(_seg_sum(kc * kc, lane_lo)), 1e-12)
        av_ref[:, cs] = -kc
        bv_ref[:, cs] = kc * a[:, cs]
    km_ref[...] = k * (1.0 + (a - 1.0) * ka_ref[...])


def rwkv_prep(p, k_col, lora, w2, a2, w0, a0, k_k, k_a, L, tm=512, tn=512):
    m = p.shape[0]
    dm = w2.shape[1]
    lp = RWKV_LORA_PAD
    tm = min(tm, m)
    assert tm % L == 0
    k_off = k_col // tn
    par = pl.BlockSpec((1, tn), lambda i, j: (0, j))
    out = pl.BlockSpec((tm, tn), lambda i, j: (i, j))
    row = lambda a: a.reshape(1, dm).astype(F32)
    return pl.pallas_call(
        functools.partial(_rwkv_prep_kernel, L=L),
        grid=(m // tm, dm // tn),
        in_specs=[pl.BlockSpec((tm, tn), lambda i, j: (i, k_off + j)),
                  pl.BlockSpec((tm, lp), lambda i, j: (i, 0)),
                  pl.BlockSpec((tm, lp), lambda i, j: (i, 1)),
                  pl.BlockSpec((lp, tn), lambda i, j: (0, j)),
                  pl.BlockSpec((lp, tn), lambda i, j: (0, j)),
                  par, par, par, par],
        out_specs=[out] * 5,
        out_shape=[jax.ShapeDtypeStruct((m, dm), F32)] * 5,
        compiler_params=_cparams("parallel", "parallel"),
        name="rwkv_prep",
    )(p, lora, lora, w2, a2, row(w0), row(a0), row(k_k), row(k_a))


def _rwkv_kernel(r_ref, ld_ref, lc_ref, k_ref, v_ref, a_ref, b_ref, gate_ref, rk_ref, lg_ref, lb_ref, s0_ref,
                 y_ref, sout_ref, sp_ref, *, L, Gp, Ns, nC):
    hd = RWKV_HEAD_DIM
    G = 2 * Gp
    R = G * L
    W = Gp * LANES
    tstep = pl.program_id(2)
    bm = (lax.broadcasted_iota(jnp.int32, (LANES, LANES), 0) // hd
          == lax.broadcasted_iota(jnp.int32, (LANES, LANES), 1) // hd)

    @pl.when(tstep == 0)
    def _():
        for p in range(Ns * Gp):
            x = s0_ref[0, p]
            sp_ref[p] = jnp.where(bm, jnp.concatenate([x, x], axis=1), 0.0)

    head_mask = (lax.broadcasted_iota(jnp.int32, (R, W), 0) // L
                 == lax.broadcasted_iota(jnp.int32, (R, W), 1) // hd)
    ri = lax.broadcasted_iota(jnp.int32, (R, R), 0)
    ci = lax.broadcasted_iota(jnp.int32, (R, R), 1)
    same = (ri // L) == (ci // L)
    strict = same & (ci < ri)
    ri2 = lax.broadcasted_iota(jnp.int32, (R, 2 * R), 0)
    ci2 = lax.broadcasted_iota(jnp.int32, (R, 2 * R), 1)
    ci2 = jnp.where(ci2 >= R, ci2 - R, ci2)
    incl2 = ((ri2 // L) == (ci2 // L)) & (ci2 <= ri2)
    lo = (lax.broadcasted_iota(jnp.int32, (L, LANES), 1) < hd)
    n_dbl = max(1, (L - 1).bit_length())
    stack = lambda x: jnp.where(head_mask, jnp.concatenate([x] * G, axis=0), 0.0).astype(BF16)

    streams = range(Ns)

    def chunk(c, carry):
        r0 = pl.multiple_of(c * L, L)
        ld_ = lambda ref, s: ref[0, pl.ds(r0, L), s * W:(s + 1) * W]
        ar2, bk2, v2, bkc, c_last = [], [], [], [], []
        for s in streams:
            r, ldec, lc, k, v, av, bv = (ld_(r_ref, s), ld_(ld_ref, s), ld_(lc_ref, s), ld_(k_ref, s),
                                         ld_(v_ref, s), ld_(a_ref, s), ld_(b_ref, s))
            lcl = lc[L - 1:L, :]
            e_neg = jnp.exp(-lc)
            e_rem = jnp.exp(lcl - lc)
            ar2.append(jnp.concatenate([stack(av * jnp.exp(lc - ldec)), stack(r * jnp.exp(lc))], axis=0))
            bk2.append(jnp.concatenate([stack(bv * e_neg), stack(k * e_neg)], axis=0))
            v2.append(stack(v))
            bkc.append(jnp.concatenate([stack(bv * e_rem), stack(k * e_rem)], axis=0))
            c_last.append(jnp.exp(lcl))
        g = [pl.dot(ar2[s], bk2[s], trans_b=True) for s in streams]
        nmat = [jnp.where(strict, g[s][:R, :R], 0.0).astype(BF16) for s in streams]
        a_ak = [jnp.where(strict, g[s][:R, R:], 0.0).astype(BF16) for s in streams]
        a_rbk = [jnp.where(incl2, g[s][R:, :], 0.0).astype(BF16) for s in streams]
        akv = [pl.dot(a_ak[s], v2[s]) for s in streams]
        minv = [nmat[s].astype(F32) for s in streams]
        for it in range(1, n_dbl):
            nmat = [pl.dot(nmat[s], nmat[s]).astype(BF16) for s in streams]
            minv = [minv[s] + nmat[s] + pl.dot(nmat[s], minv[s].astype(BF16)) for s in streams]
        ur0 = [jnp.concatenate(
            [pl.dot(ar2[s][:, p * LANES:(p + 1) * LANES], sp_ref[s * Gp + p].astype(BF16), trans_b=True)
             for p in range(Gp)], axis=1) for s in streams]
        x = [ur0[s][:R] + akv[s] for s in streams]
        x = [x[s] + pl.dot(minv[s].astype(BF16), x[s].astype(BF16)) for s in streams]
        xv = [jnp.concatenate([x[s].astype(BF16), v2[s]], axis=0) for s in streams]
        y2 = [ur0[s][R:] + pl.dot(a_rbk[s], xv[s]) for s in streams]
        for s in streams:
            y = y2[s][0:L]
            for j in range(1, G):
                y = y + y2[s][j * L:(j + 1) * L]
            for p in range(Gp):
                cs = slice(p * LANES, (p + 1) * LANES)
                gs = slice(s * W + p * LANES, s * W + (p + 1) * LANES)
                q = s * Gp + p
                sp_ref[q] = sp_ref[q] * c_last[s][:, cs] + pl.dot(xv[s][:, cs], bkc[s][:, cs], trans_a=True)
                yp = y[:, cs]
                rp, kp, vp = r_ref[0, pl.ds(r0, L), gs], k_ref[0, pl.ds(r0, L), gs], v_ref[0, pl.ds(r0, L), gs]
                mean = _seg_sum(yp, lo) * (1.0 / hd)
                yc = yp - mean
                var = _seg_sum(yc * yc, lo) * (1.0 / hd)
                yn = yc * lax.rsqrt(var + RWKV_GN_EPS) * lg_ref[:, gs] + lb_ref[:, gs]
                bonus = _seg_sum(rp * kp * rk_ref[:, gs], lo) * vp
                y_ref[0, pl.ds(r0, L), gs] = ((yn + bonus) * _silu(gate_ref[0, pl.ds(r0, L), gs])).astype(y_ref.dtype)
        return carry

    lax.fori_loop(0, nC, chunk, 0)

    @pl.when(tstep == pl.num_programs(2) - 1)
    def _():
        row_lo = lax.broadcasted_iota(jnp.int32, (LANES, hd), 0) < hd
        for p in range(Ns * Gp):
            sp = sp_ref[p]
            sout_ref[0, p] = jnp.where(row_lo, sp[:, :hd], sp[:, hd:])


def rwkv_mix(r_src, r_col, v_col, g_col, ld, lc, km, av, bv, b, t, L, r_k, ln_g, ln_b, s0):
    heads, hd = s0.shape[1], s0.shape[2]
    dm = heads * hd
    pairs = heads // 2
    gp = max(g for g in range(1, pairs + 1)
             if pairs % g == 0 and 2 * g * L <= RWKV_ROWS and (2 * g * L) % LANES == 0)
    ns = max(n for n in range(1, RWKV_STREAMS + 1) if (pairs // gp) % n == 0)
    tb = min(t, RWKV_SEQ_BLOCK)
    w = ns * gp * LANES
    assert r_col % w == 0 and v_col % w == 0 and g_col % w == 0
    r_off, v_off, g_off = r_col // w, v_col // w, g_col // w
    three = lambda a: a.reshape(b, t, a.shape[1])
    row = lambda a: a.reshape(1, dm).astype(F32)
    seq = lambda off: pl.BlockSpec((1, tb, w), lambda i, j, tt: (i, tt, off + j))
    par = pl.BlockSpec((1, w), lambda i, j, tt: (0, j))
    st_spec = pl.BlockSpec((1, ns * gp, LANES, hd), lambda i, j, tt: (i, j, 0, 0))
    s0p = s0.reshape(b, pairs, LANES, hd)
    y, s_out = pl.pallas_call(
        functools.partial(_rwkv_kernel, L=L, Gp=gp, Ns=ns, nC=tb // L),
        grid=(b, pairs // (gp * ns), t // tb),
        in_specs=[seq(r_off), seq(0), seq(0), seq(0), seq(v_off), seq(0), seq(0), seq(g_off), par, par, par, st_spec],
        out_specs=[seq(0), st_spec],
        out_shape=[jax.ShapeDtypeStruct((b, t, dm), BF16), jax.ShapeDtypeStruct(s0p.shape, F32)],
        scratch_shapes=[pltpu.VMEM((ns * gp, LANES, LANES), F32)],
        compiler_params=_cparams("parallel", "parallel", "arbitrary"),
        name="rwkv_mix",
    )(three(r_src), three(ld), three(lc), three(km), three(r_src), three(av), three(bv), three(r_src),
      row(r_k), row(ln_g), row(ln_b), s0p)
    return y.reshape(b * t, dm), s_out.reshape(s0.shape)


def _hgrn_lower_bounds(logits):
    p = jax.nn.softmax(logits.astype(F32), axis=0)
    return jnp.maximum(jnp.cumsum(p, axis=0) - p[0], 0.0)


def _mlstm_weights(w_in):
    dq = MLSTM_HEADS * MLSTM_DQK
    o = [0, dq, 2 * dq, 2 * dq + D_MIX, 2 * dq + D_MIX + MLSTM_HEADS, 2 * dq + D_MIX + 2 * MLSTM_HEADS]
    q, k, v = w_in[:, o[0]:o[1]], w_in[:, o[1]:o[2]], w_in[:, o[2]:o[3]]
    gates = w_in[:, o[3]:o[5]]
    rest = w_in[:, o[5]:]
    pad = lambda a: jnp.pad(a.reshape(-1, MLSTM_HEADS, MLSTM_DQK),
                            ((0, 0), (0, 0), (0, MLSTM_DQK_PAD - MLSTM_DQK))).reshape(-1, MLSTM_HEADS * MLSTM_DQK_PAD)
    main = jnp.concatenate([v, rest, pad(q), pad(k)], axis=1).astype(BF16)
    gates = jnp.pad(gates, ((0, 0), (0, LANES - 2 * MLSTM_HEADS))).astype(BF16)
    return main, gates


def _rwkv_weights(w_in, mu):
    c = [0]
    for n in (D_MIX, RWKV_LORA, D_MIX, D_MIX, RWKV_LORA, D_GATE, D_XATTN):
        c.append(c[-1] + n)
    sl = lambda i: w_in[:, c[i]:c[i + 1]]
    main = jnp.concatenate([sl(0), sl(2), sl(3), sl(5), sl(6)], axis=1).astype(BF16)
    padl = lambda a: jnp.pad(a, ((0, 0), (0, RWKV_LORA_PAD - RWKV_LORA)))
    lora = jnp.concatenate([padl(sl(1)), padl(sl(4))], axis=1).astype(BF16)
    mu_ext = jnp.concatenate([mu.astype(F32), jnp.zeros((1, D_MODEL), F32)], axis=0)
    tile = 1024
    main_groups = ([0] * (D_MIX // tile) + [2] * (D_MIX // tile) + [3] * (D_MIX // tile)
                   + [5] * (D_GATE // tile) + [6] * (D_XATTN // tile))
    return main, lora, mu_ext, main_groups, [1, 4]


def _trunk(x3, mk, mv, mem_cols, s_hgrn, s_mc, s_mn, s_mm, s_rwkv, s_shift, prm, hgrn_hb):
    b, t, d = x3.shape
    x = x3.reshape(b * t, d)
    out = dict(hgrn=[], mc=[], mn=[], mm=[], rwkv=[], shift=[])
    for i in range(DEPTH):
        kind, j = i % N_MIXERS, i // N_MIXERS
        if kind == 0:
            h = rmsnorm(x, prm['norm_pre'][i], BF16)
            p = matmul(h, prm['hgrn_w'], layer=j, tn=512)
            gate_col, q_col = 3 * D_MIX, 3 * D_MIX + D_GATE
            mix, s = hgrn_mix(p, gate_col, b, t, prm['hgrn_lb'][j], prm['hgrn_norm'][j], s_hgrn, j, hgrn_hb)
            out['hgrn'].append(s)
        elif kind == 1:
            h = rmsnorm(x, prm['norm_pre'][i], BF16)
            w_main, w_gates = prm['mlstm_w'][j]
            p = matmul(h, w_main)
            gates = matmul(h, w_gates, tn=LANES)[:, :2 * MLSTM_HEADS]
            gate_col, q_col = 2 * D_MIX, 2 * D_MIX + D_GATE
            mix, c, n, m = mlstm_mix(p, gates, b, t, prm['mlstm_b_gate'][j], prm['mlstm_norm'][j],
                                     s_mc, s_mn, s_mm, j)
            out['mc'].append(c)
            out['mn'].append(n)
            out['mm'].append(m)
        else:
            h = rmsnorm(x, prm['norm_pre'][i], F32)
            h3 = h.reshape(b, t, d)
            hp = jnp.concatenate([s_shift[j][:, None, :], h3[:, :-1]], axis=1).reshape(b * t, d)
            w_main, w_lora, mu_ext, main_groups, lora_groups = prm['rwkv_w'][j]
            mixes = token_shift_lerp(h, hp, mu_ext)
            p = matmul_grouped(mixes, w_main, main_groups)
            lora = matmul_grouped(mixes, w_lora, lora_groups, tn=RWKV_LORA_PAD)
            L = math.gcd(t, RWKV_CHUNK)
            ld, lc, km, av, bv = rwkv_prep(p, D_MIX, lora, prm['rwkv_w2'][j], prm['rwkv_a2'][j], prm['rwkv_w0'][j],
                                           prm['rwkv_a0'][j], prm['rwkv_k_k'][j], prm['rwkv_k_a'][j], L)
            gate_col, q_col = 3 * D_MIX, 3 * D_MIX + D_GATE
            mix, s = rwkv_mix(p, 0, 2 * D_MIX, gate_col, ld, lc, km, av, bv, b, t, L, prm['rwkv_r_k'][j],
                              prm['rwkv_ln_g'][j], prm['rwkv_ln_b'][j], s_rwkv[j])
            out['rwkv'].append(s)
            out['shift'].append(h3[:, -1])
        lyr, k_col, v_col = mem_cols(i)
        if isinstance(mk, list):
            xo = memattn(p, q_col, gate_col + D_MIX, mk[i], mv[i], lyr, k_col, v_col, b, t, X_HEADS)
        else:
            xo = memattn_cache(p, q_col, gate_col + D_MIX, mk, mv, lyr, b, t)
        x = outproj(mix, xo, prm['w_out'], i, x, prm['norm_post'][i])
    cat = lambda parts: parts[0] if len(parts) == 1 else jnp.concatenate(parts, axis=0)
    return (x.reshape(b, t, d), jnp.stack(out['hgrn']), cat(out['mc']), cat(out['mn']), cat(out['mm']),
            jnp.stack(out['rwkv']), jnp.stack(out['shift']))


def kernel(x_prompt, x_sample, mem_prompt, cache_mem_k, cache_mem_v, state_hgrn, state_mlstm_c, state_mlstm_n, state_mlstm_m, state_rwkv, state_rwkv_shift, norm_pre, norm_post, norm_mem, w_mem_kv, w_out, hgrn_w_in, hgrn_lb_logits, hgrn_norm, mlstm_w_in, mlstm_b_gate, mlstm_norm, rwkv_w_in, rwkv_mu, rwkv_w0, rwkv_w2, rwkv_a0, rwkv_a2, rwkv_k_k, rwkv_k_a, rwkv_r_k, rwkv_ln_g, rwkv_ln_b):
    n_h, n_m, n_r = hgrn_w_in.shape[0], mlstm_w_in.shape[0], rwkv_w_in.shape[0]
    pad_lora = lambda a: jnp.pad(a, ((0, RWKV_LORA_PAD - RWKV_LORA), (0, 0))).astype(BF16)
    prm = dict(
        norm_pre=norm_pre, norm_post=norm_post, w_out=w_out.astype(BF16),
        hgrn_w=hgrn_w_in,
        hgrn_lb=_hgrn_lower_bounds(hgrn_lb_logits), hgrn_norm=hgrn_norm,
        mlstm_w=[_mlstm_weights(mlstm_w_in[j]) for j in range(n_m)],
        mlstm_b_gate=mlstm_b_gate, mlstm_norm=mlstm_norm,
        rwkv_w=[_rwkv_weights(rwkv_w_in[j], rwkv_mu[j]) for j in range(n_r)],
        rwkv_w0=rwkv_w0, rwkv_w2=[pad_lora(rwkv_w2[j]) for j in range(n_r)],
        rwkv_a0=rwkv_a0, rwkv_a2=[pad_lora(rwkv_a2[j]) for j in range(n_r)],
        rwkv_k_k=rwkv_k_k, rwkv_k_a=rwkv_k_a, rwkv_r_k=rwkv_r_k, rwkv_ln_g=rwkv_ln_g, rwkv_ln_b=rwkv_ln_b)

    bp, n_mem, d = mem_prompt.shape
    mem2 = mem_prompt.reshape(bp * n_mem, d)
    kvs = [matmul(rmsnorm(mem2, norm_mem[i], BF16), w_mem_kv[i].astype(BF16)).reshape(1, bp, n_mem, 2 * D_XATTN)
           for i in range(DEPTH)]
    heads_shape = (bp, n_mem, X_HEADS, X_HEAD_DIM)
    mem_k_prompt = jnp.stack([kv[0, :, :, :D_XATTN].reshape(heads_shape) for kv in kvs])
    mem_v_prompt = jnp.stack([kv[0, :, :, D_XATTN:].reshape(heads_shape) for kv in kvs])

    dt = x_prompt.dtype
    zeros = lambda *shape: jnp.zeros(shape, dt)
    (y_prompt, hgrn_p, mc_p, mn_p, mm_p, rwkv_p, shift_p) = _trunk(
        x_prompt, kvs, kvs, lambda i: (0, 0, D_XATTN),
        zeros(n_h, bp, HGRN_HEADS, HGRN_HEAD_DIM, HGRN_HEAD_DIM),
        zeros(n_m, bp, MLSTM_HEADS, MLSTM_DQK, MLSTM_DV),
        zeros(n_m, bp, MLSTM_HEADS, MLSTM_DQK),
        zeros(n_m, bp, MLSTM_HEADS),
        zeros(n_r, bp, RWKV_HEADS, RWKV_HEAD_DIM, RWKV_HEAD_DIM),
        zeros(n_r, bp, D_MODEL),
        prm, hgrn_hb=6)
    (y_sample, hgrn_s, mc_s, mn_s, mm_s, rwkv_s, shift_s) = _trunk(
        x_sample, cache_mem_k, cache_mem_v, lambda i: (i, 0, 0), state_hgrn, state_mlstm_c, state_mlstm_n, state_mlstm_m,
        state_rwkv, state_rwkv_shift, prm, hgrn_hb=8)
    return (y_prompt, y_sample, mem_k_prompt, mem_v_prompt,
            hgrn_p, mc_p, mn_p, mm_p, rwkv_p, shift_p,
            hgrn_s, mc_s, mn_s, mm_s, rwkv_s, shift_s)
```

```python
import functools
import math

import jax
import jax.numpy as jnp
from jax import lax
from jax.experimental import pallas as pl
from jax.experimental.pallas import tpu as pltpu

F32 = jnp.float32
BF16 = jnp.bfloat16

D_MODEL = 4096
DEPTH = 4
N_MIXERS = 3
D_MIX = 3 * D_MODEL // 4
D_XATTN = D_MODEL // 4
D_GATE = D_MIX + D_XATTN
N_MEM = 256
X_HEADS = 4
X_HEAD_DIM = D_XATTN // X_HEADS

HGRN_HEAD_DIM = 128
HGRN_HEADS = D_MIX // HGRN_HEAD_DIM
HGRN_CHUNK = 16

MLSTM_HEADS = 8
MLSTM_DV = D_MIX // MLSTM_HEADS
MLSTM_DQK = MLSTM_DV // 2
MLSTM_DQK_PAD = 256
MLSTM_CHUNK = 64

RWKV_HEAD_DIM = 64
RWKV_HEADS = D_MIX // RWKV_HEAD_DIM
RWKV_LORA = max(32, int(round(1.8 * math.sqrt(D_MIX) / 32)) * 32)
RWKV_LORA_PAD = 128
RWKV_GN_EPS = 64e-5
RWKV_CHUNK = 64
RWKV_ROWS = 256
RWKV_STREAMS = 3
RWKV_SEQ_BLOCK = 512

NORM_EPS = 1e-6
LOG2E = 1.4426950408889634
LANES = 128
SEQ_BLOCK = 512
HGRN_SEQ_BLOCK = 1024
VMEM_LIMIT = 48 * 1024 * 1024
VMEM_LIMIT_OUTPROJ = 56 * 1024 * 1024


def _cparams(*sem, vmem=VMEM_LIMIT):
    return pltpu.CompilerParams(dimension_semantics=sem, vmem_limit_bytes=vmem)


def _mm(a, b, **kw):
    return pl.dot(a.astype(BF16), b.astype(BF16), **kw)


def _sigmoid(x):
    return 1.0 / (1.0 + jnp.exp(-x))


def _silu(g):
    return g * _sigmoid(g)


def _log_sigmoid(x):
    return jnp.minimum(x, 0.0) - jnp.log1p(jnp.exp(-jnp.abs(x)))


def _cumsum_rows(x, block):
    row = lax.broadcasted_iota(jnp.int32, x.shape, 0) & (block - 1)
    sh = 1
    while sh < block:
        x = x + jnp.where(row >= sh, pltpu.roll(x, sh, axis=0), 0.0)
        sh *= 2
    return x


def _rmsnorm_kernel(x_ref, g_ref, o_ref):
    x = x_ref[...]
    ms = jnp.mean(x * x, axis=-1, keepdims=True)
    o_ref[...] = ((x * lax.rsqrt(ms + NORM_EPS)) * g_ref[...]).astype(o_ref.dtype)


def rmsnorm(x, gain, out_dtype, tm=256):
    m, d = x.shape
    tm = min(tm, m)
    return pl.pallas_call(
        _rmsnorm_kernel,
        grid=(m // tm,),
        in_specs=[pl.BlockSpec((tm, d), lambda i: (i, 0)), pl.BlockSpec((1, d), lambda i: (0, 0))],
        out_specs=pl.BlockSpec((tm, d), lambda i: (i, 0)),
        out_shape=jax.ShapeDtypeStruct((m, d), out_dtype),
        compiler_params=_cparams("parallel"),
        name="rmsnorm",
    )(x, gain.reshape(1, d))


def _mm_kernel(a_ref, w_ref, o_ref):
    o_ref[...] = _mm(a_ref[...], w_ref[...])


def matmul(a, w, layer=None, tm=1024, tn=1024):
    m, k = a.shape
    n = w.shape[-1]
    tm, tn = min(tm, m), min(tn, n)
    if layer is None:
        w_spec = pl.BlockSpec((k, tn), lambda i, j: (0, j))
    else:
        w_spec = pl.BlockSpec((None, k, tn), lambda i, j: (layer, 0, j))
    return pl.pallas_call(
        _mm_kernel,
        grid=(m // tm, n // tn),
        in_specs=[pl.BlockSpec((tm, k), lambda i, j: (i, 0)), w_spec],
        out_specs=pl.BlockSpec((tm, tn), lambda i, j: (i, j)),
        out_shape=jax.ShapeDtypeStruct((m, n), F32),
        compiler_params=_cparams("parallel", "parallel"),
        name="matmul",
    )(a, w)


def _lerp_kernel(h_ref, hp_ref, mu_ref, o_ref):
    h = h_ref[...]
    d = hp_ref[...] - h
    for g in range(o_ref.shape[0]):
        o_ref[g] = (h + d * mu_ref[g]).astype(o_ref.dtype)


def token_shift_lerp(h, hp, mu, tm=128):
    m, k = h.shape
    g = mu.shape[0]
    tm = min(tm, m)
    return pl.pallas_call(
        _lerp_kernel,
        grid=(m // tm,),
        in_specs=[pl.BlockSpec((tm, k), lambda i: (i, 0)),
                  pl.BlockSpec((tm, k), lambda i: (i, 0)),
                  pl.BlockSpec((g, 1, k), lambda i: (0, 0, 0))],
        out_specs=pl.BlockSpec((g, tm, k), lambda i: (0, i, 0)),
        out_shape=jax.ShapeDtypeStruct((g, m, k), BF16),
        compiler_params=_cparams("parallel"),
        name="token_shift_lerp",
    )(h, hp, mu.reshape(g, 1, k).astype(F32))


def matmul_grouped(a, w, tile_group, tm=1024, tn=1024):
    _, m, k = a.shape
    n = w.shape[1]
    tm, tn = min(tm, m), min(tn, n)
    assert len(tile_group) == n // tn and list(tile_group) == sorted(tile_group)

    def group(j):
        g = tile_group[0]
        for t in range(1, len(tile_group)):
            if tile_group[t] != tile_group[t - 1]:
                g = g + jnp.where(j >= t, tile_group[t] - tile_group[t - 1], 0)
        return g

    return pl.pallas_call(
        _mm_kernel,
        grid=(m // tm, n // tn),
        in_specs=[pl.BlockSpec((None, tm, k), lambda i, j: (group(j), i, 0)),
                  pl.BlockSpec((k, tn), lambda i, j: (0, j))],
        out_specs=pl.BlockSpec((tm, tn), lambda i, j: (i, j)),
        out_shape=jax.ShapeDtypeStruct((m, n), F32),
        compiler_params=_cparams("parallel", "parallel"),
        name="matmul_grouped",
    )(a, w)


def _outproj_kernel(mix_ref, xo_ref, w_ref, x_ref, gain_ref, ngain_ref, o_ref, *h_ref):
    d_mix = mix_ref.shape[1]
    y = pl.dot(mix_ref[...], w_ref[0:d_mix, :]) + pl.dot(xo_ref[...], w_ref[d_mix:, :])
    ms = jnp.mean(y * y, axis=-1, keepdims=True)
    x = x_ref[...] + (y * lax.rsqrt(ms + NORM_EPS)) * gain_ref[...]
    o_ref[...] = x
    if h_ref:
        ms = jnp.mean(x * x, axis=-1, keepdims=True)
        h_ref[0][...] = ((x * lax.rsqrt(ms + NORM_EPS)) * ngain_ref[...]).astype(h_ref[0].dtype)


def outproj(u_mix, u_xo, w_all, layer, x, gain, next_gain=None, next_dtype=None, tm=128):
    m, d_mix = u_mix.shape
    d_x = u_xo.shape[1]
    kdim, n = w_all.shape[1:]
    tm = min(tm, m)
    assert kdim == d_mix + d_x
    rows = pl.BlockSpec((tm, n), lambda i: (i, 0))
    vec = pl.BlockSpec((1, n), lambda i: (0, 0))
    fused = next_gain is not None
    out = pl.pallas_call(
        _outproj_kernel,
        grid=(m // tm,),
        in_specs=[pl.BlockSpec((tm, d_mix), lambda i: (i, 0)),
                  pl.BlockSpec((tm, d_x), lambda i: (i, 0)),
                  pl.BlockSpec((None, kdim, n), lambda i: (layer, 0, 0), pipeline_mode=pl.Buffered(1)),
                  rows, vec, vec],
        out_specs=[rows, rows] if fused else [rows],
        out_shape=[jax.ShapeDtypeStruct((m, n), F32)] + ([jax.ShapeDtypeStruct((m, n), next_dtype)] if fused else []),
        compiler_params=_cparams("parallel", vmem=VMEM_LIMIT_OUTPROJ),
        name="outproj",
    )(u_mix, u_xo, w_all, x, gain.reshape(1, n), (next_gain if fused else gain).reshape(1, n))
    return (out[0], out[1]) if fused else (out[0], None)


def _memattn_kernel(q_ref, g_ref, k_ref, v_ref, o_ref, *, heads, hd, bb):
    scale = hd ** -0.5
    for i in range(bb):
        for h in range(heads):
            cs = slice(h * hd, (h + 1) * hd)
            s = _mm(q_ref[i, :, cs], k_ref[i, :, cs], trans_b=True) * scale
            s = s - jnp.max(s, axis=-1, keepdims=True)
            e = jnp.exp(s)
            p = e / jnp.sum(e, axis=-1, keepdims=True)
            o_ref[i, :, cs] = (_mm(p, v_ref[i, :, cs]) * _silu(g_ref[i, :, cs])).astype(o_ref.dtype)


def memattn(p, q_col, g_col, mk, mv, layer, k_col, v_col, b, t, heads, tq=512, batch_block=4):
    n_mem = mk.shape[2]
    w = D_XATTN
    hd = w // heads
    tq = min(tq, t)
    bb = batch_block if tq == t and b % batch_block == 0 else 1
    assert q_col % w == 0 and g_col % w == 0 and k_col % w == 0 and v_col % w == 0
    q_off, g_off, k_off, v_off = q_col // w, g_col // w, k_col // w, v_col // w
    p3 = p.reshape(b, t, p.shape[1])
    out = pl.pallas_call(
        functools.partial(_memattn_kernel, heads=heads, hd=hd, bb=bb),
        grid=(b // bb, t // tq),
        in_specs=[pl.BlockSpec((bb, tq, w), lambda i, j: (i, j, q_off)),
                  pl.BlockSpec((bb, tq, w), lambda i, j: (i, j, g_off)),
                  pl.BlockSpec((None, bb, n_mem, w), lambda i, j: (layer, i, 0, k_off)),
                  pl.BlockSpec((None, bb, n_mem, w), lambda i, j: (layer, i, 0, v_off))],
        out_specs=pl.BlockSpec((bb, tq, w), lambda i, j: (i, j, 0)),
        out_shape=jax.ShapeDtypeStruct((b, t, w), BF16),
        compiler_params=_cparams("parallel", "parallel"),
        name="memattn",
    )(p3, p3, mk, mv)
    return out.reshape(b * t, w)


def _memattn_cache_kernel(q_ref, g_ref, k_ref, v_ref, o_ref, *, heads, rows, bb):
    n_col = k_ref.shape[1]
    scale = (2 * LANES) ** -0.5
    col = lax.broadcasted_iota(jnp.int32, (rows, n_col), 1)
    same_head = (col % heads) == (lax.broadcasted_iota(jnp.int32, (rows, n_col), 0) % heads)
    half = (col // heads) % 2
    keep = [same_head & (half == 0), same_head & (half == 1)]
    for i in range(bb):
        k = k_ref[i].astype(BF16)
        v = v_ref[i].astype(BF16)
        s = [jnp.where(keep[c], _mm(q_ref[i, c], k, trans_b=True), 0.0) for c in range(2)]
        s = s[0] + s[1]
        s = s + pltpu.roll(s, n_col - heads, axis=1)
        s = jnp.where(keep[0], s * scale, -jnp.inf)
        e = jnp.exp(s - jnp.max(s, axis=-1, keepdims=True))
        p = e / jnp.sum(e, axis=-1, keepdims=True)
        for c in range(2):
            pc = p if c == 0 else pltpu.roll(p, heads, axis=1)
            o_ref[i, c] = (pl.dot(pc.astype(BF16), v) * _silu(g_ref[i, c])).astype(o_ref.dtype)


def memattn_cache(p, q_col, g_col, cache_k, cache_v, layer, b, t, batch_block=4):
    heads = cache_k.shape[3]
    assert cache_k.shape[4] == 2 * LANES
    n_mem = cache_k.shape[2]
    rows, w = t * heads, heads * 2 * LANES
    bb = batch_block if b % batch_block == 0 else 1

    def rows_view(a):
        a = a.reshape(a.shape[0], b, n_mem, heads, 2, LANES).transpose(0, 1, 2, 4, 3, 5)
        return a.reshape(a.shape[0] * b, n_mem * 2 * heads, LANES)

    def halves(a):
        return a.reshape(b, t, heads, 2, LANES).transpose(0, 3, 1, 2, 4).reshape(b, 2, rows, LANES)

    tok = pl.BlockSpec((bb, 2, rows, LANES), lambda i: (i, 0, 0, 0))
    mem = pl.BlockSpec((bb, n_mem * 2 * heads, LANES), lambda i: (layer * (b // bb) + i, 0, 0))
    out = pl.pallas_call(
        functools.partial(_memattn_cache_kernel, heads=heads, rows=rows, bb=bb),
        grid=(b // bb,),
        in_specs=[tok, tok, mem, mem],
        out_specs=tok,
        out_shape=jax.ShapeDtypeStruct((b, 2, rows, LANES), BF16),
        compiler_params=_cparams("parallel"),
        name="memattn_cache",
    )(halves(p[:, q_col:q_col + w]), halves(p[:, g_col:g_col + w]), rows_view(cache_k), rows_view(cache_v))
    return out.reshape(b, 2, t, heads, LANES).transpose(0, 2, 3, 1, 4).reshape(b * t, w)


def _hgrn_kernel(q_ref, f_ref, i_ref, gate_ref, loglb_ref, l1m_ref, oml_ref, g_ref, s0_ref, *rest, L, Hb, nC):
    o_ref, sout_ref, st_ref = rest[-3:]
    hd = HGRN_HEAD_DIM
    tstep = pl.program_id(2)

    @pl.when(tstep == 0)
    def _():
        for hh in range(Hb):
            st_ref[hh] = s0_ref[0, hh].T

    sub = 8
    row = lax.broadcasted_iota(jnp.int32, (sub, hd), 0)
    lane = lax.broadcasted_iota(jnp.int32, (sub, hd), 1)

    def gates(c):
        r0 = pl.multiple_of(c * L, L)
        staged = []
        for hh in range(Hb):
            cs = slice(hh * hd, (hh + 1) * hd)
            q = q_ref[0, pl.ds(r0, L), cs]
            f = f_ref[0, pl.ds(r0, L), cs]
            vh = i_ref[0, pl.ds(r0, L), cs]
            qh = q * _sigmoid(q)
            a = loglb_ref[:, cs]
            cc = l1m_ref[:, cs] + _log_sigmoid(f)
            lf = jnp.maximum(a, cc) + jnp.log1p(jnp.exp(-jnp.abs(a - cc)))
            kh = oml_ref[:, cs] * _sigmoid(-f)
            b = _cumsum_rows(lf, L) * LOG2E
            pieces = []
            for r0p in range(0, L, sub):
                bp, qp = b[r0p:r0p + sub, :], qh[r0p:r0p + sub, :]
                attn = jnp.zeros((sub, hd), F32)
                for s in range(min(L, r0p + sub)):
                    dec = jnp.exp2(bp - b[s:s + 1, :])
                    if s > r0p:
                        dec = jnp.where(row >= s - r0p, dec, 0.0)
                    col = jnp.sum((qp * kh[s:s + 1, :]) * dec, axis=-1, keepdims=True)
                    attn = jnp.where(lane == s, col, attn)
                pieces.append(attn)
            attn = pieces[0] if len(pieces) == 1 else jnp.concatenate(pieces, axis=0)
            b_last = b[L - 1:L, :]
            staged.append(((qh * jnp.exp2(b)).astype(BF16), attn[:, :L].astype(BF16), vh.astype(BF16),
                           (kh * jnp.exp2(b_last - b)).astype(BF16), jnp.exp2(b_last)))
        return tuple(staged)

    def update(c, staged):
        r0 = pl.multiple_of(c * L, L)
        for hh in range(Hb):
            cs = slice(hh * hd, (hh + 1) * hd)
            qe, attn, vh, kd, e_last = staged[hh]
            st = st_ref[hh]
            o = pl.dot(qe, st.astype(BF16), trans_b=True) + pl.dot(attn, vh)
            o = o * lax.rsqrt(jnp.mean(o * o, axis=-1, keepdims=True) + NORM_EPS)
            o_ref[0, pl.ds(r0, L), cs] = ((o * g_ref[:, cs]) * _silu(gate_ref[0, pl.ds(r0, L), cs])).astype(o_ref.dtype)
            st_ref[hh] = st * e_last + pl.dot(vh, kd, trans_a=True)

    def body(c, staged):
        nxt = gates(c + 1)
        update(c, staged)
        return nxt

    update(nC - 1, lax.fori_loop(0, nC - 1, body, gates(0)))

    @pl.when(tstep == pl.num_programs(2) - 1)
    def _():
        for hh in range(Hb):
            sout_ref[0, hh] = st_ref[hh].T


def hgrn_mix(p, gate_col, b, t, lb, norm_g, s_all, layer, hb, s_new=None):
    heads, hd = s_all.shape[2], s_all.shape[3]
    dm = heads * hd
    L = math.gcd(t, HGRN_CHUNK)
    tb = min(t, HGRN_SEQ_BLOCK)
    w = hb * hd
    nhb = heads // hb
    assert gate_col % w == 0
    p3 = p.reshape(b, t, p.shape[1])
    row = lambda a: a.reshape(1, dm).astype(F32)
    par_spec = pl.BlockSpec((1, w), lambda i, j, tt: (0, j))
    st_spec = pl.BlockSpec((None, 1, hb, hd, hd), lambda i, j, tt: (layer, i, j, 0, 0))
    args = [p3, p3, p3, p3, row(jnp.log(lb)), row(jnp.log1p(-lb)), row(1.0 - lb), row(norm_g), s_all]
    in_specs = [pl.BlockSpec((1, tb, w), lambda i, j, tt: (i, tt, j)),
                pl.BlockSpec((1, tb, w), lambda i, j, tt: (i, tt, nhb + j)),
                pl.BlockSpec((1, tb, w), lambda i, j, tt: (i, tt, 2 * nhb + j)),
                pl.BlockSpec((1, tb, w), lambda i, j, tt: (i, tt, gate_col // w + j)),
                par_spec, par_spec, par_spec, par_spec, st_spec]
    aliases = {}
    if s_new is not None:
        args.append(s_new)
        in_specs.append(pl.BlockSpec(memory_space=pl.ANY))
        aliases = {len(args) - 1: 1}
    mix, s_out = pl.pallas_call(
        functools.partial(_hgrn_kernel, L=L, Hb=hb, nC=tb // L),
        grid=(b, nhb, t // tb),
        in_specs=in_specs,
        out_specs=[pl.BlockSpec((1, tb, w), lambda i, j, tt: (i, tt, j)), st_spec],
        out_shape=[jax.ShapeDtypeStruct((b, t, dm), BF16), jax.ShapeDtypeStruct(s_all.shape, F32)],
        scratch_shapes=[pltpu.VMEM((hb, hd, hd), F32)],
        input_output_aliases=aliases,
        compiler_params=_cparams("parallel", "parallel", "arbitrary"),
        name="hgrn_mix",
    )(*args)
    return mix.reshape(b * t, dm), s_out


def _mlstm_kernel(bg_ref, q_ref, k_ref, v_ref, op_ref, gate_ref, ig_ref, fg_ref, g_ref, c0_ref, n0_ref, m0_ref,
                  o_ref, cout_ref, nout_ref, mout_ref, c_sc, n_sc, m_sc, *, L, nC, Hb, heads, dqk, dv):
    dqp = MLSTM_DQK_PAD
    tstep = pl.program_id(2)
    h0 = pl.program_id(1) * Hb
    hs = range(Hb)

    @pl.when(tstep == 0)
    def _():
        c_sc[...] = jnp.zeros_like(c_sc)
        n_sc[...] = jnp.zeros_like(n_sc)
        for hh in hs:
            c_sc[hh, 0:dqk, :] = c0_ref[0, hh]
            n_sc[hh, :, 0:dqk] = n0_ref[0, hh]
            m_sc[hh] = m0_ref[0, hh]

    ti = lax.broadcasted_iota(jnp.int32, (L, L), 0)
    si = lax.broadcasted_iota(jnp.int32, (L, L), 1)
    eye, low = ti == si, si <= ti
    col = lambda r: jnp.sum(jnp.where(eye, r, 0.0), axis=1, keepdims=True)
    kscale = dqk ** -0.5

    def chunk(c, carry):
        r0 = pl.multiple_of(c * L, L)
        q = [q_ref[0, pl.ds(r0, L), hh * dqp:(hh + 1) * dqp] for hh in hs]
        k = [k_ref[0, pl.ds(r0, L), hh * dqp:(hh + 1) * dqp] * kscale for hh in hs]
        v = [v_ref[0, pl.ds(r0, L), hh * dv:(hh + 1) * dv] for hh in hs]
        qb = [x.astype(BF16) for x in q]
        vb = [x.astype(BF16) for x in v]
        s_qk = [pl.dot(qb[hh], k[hh].astype(BF16), trans_b=True) for hh in hs]
        q_c = [pl.dot(qb[hh], c_sc[hh].astype(BF16)) for hh in hs]
        w_inter, a, m_t, b_col, ic_col, m_prev = [], [], [], [], [], []
        for hh in hs:
            ic_row = ig_ref[0, hh, pl.ds(c, 1), :] + bg_ref[h0 + hh]
            lf_row = _log_sigmoid(fg_ref[0, hh, pl.ds(c, 1), :] + bg_ref[heads + h0 + hh])
            bc = jnp.sum(jnp.where(low, lf_row, 0.0), axis=1, keepdims=True)
            b_row = jnp.sum(jnp.where(ti <= si, col(lf_row), 0.0), axis=0, keepdims=True)
            log_w = jnp.where(low, bc - b_row + ic_row, -jnp.inf)
            m = m_sc[hh]
            log_inter = bc + m
            mt = jnp.maximum(log_inter, jnp.max(log_w, axis=1, keepdims=True))
            w_inter.append(jnp.exp(log_inter - mt))
            a.append(jnp.exp(log_w - mt) * s_qk[hh])
            m_t.append(mt)
            b_col.append(bc)
            ic_col.append(col(ic_row))
            m_prev.append(m)
        a_v = [pl.dot(a[hh].astype(BF16), vb[hh]) for hh in hs]
        kw, decay = [], []
        for hh in hs:
            m_last = m_t[hh][L - 1:L, :]
            b_last = b_col[hh][L - 1:L, :]
            kw.append(jnp.exp(b_last - b_col[hh] + ic_col[hh] - m_last) * k[hh])
            decay.append(jnp.exp(b_last + m_prev[hh] - m_last))
            m_sc[hh] = m_last
        kw_v = [pl.dot(kw[hh].astype(BF16), vb[hh], trans_a=True) for hh in hs]
        for hh in hs:
            n = n_sc[hh]
            num = w_inter[hh] * q_c[hh] + a_v[hh]
            den = w_inter[hh] * jnp.sum(q[hh] * n, axis=1, keepdims=True) + jnp.sum(a[hh], axis=1, keepdims=True)
            hid = num / jnp.maximum(jnp.abs(den), jnp.exp(-m_t[hh]))
            c_sc[hh] = decay[hh] * c_sc[hh] + kw_v[hh]
            n_sc[hh] = decay[hh] * n + jnp.sum(kw[hh], axis=0, keepdims=True)
            vs = slice(hh * dv, (hh + 1) * dv)
            hn = hid * lax.rsqrt(jnp.mean(hid * hid, axis=1, keepdims=True) + NORM_EPS) * g_ref[:, vs]
            out = (_sigmoid(op_ref[0, pl.ds(r0, L), vs]) * hn) * _silu(gate_ref[0, pl.ds(r0, L), vs])
            o_ref[0, pl.ds(r0, L), vs] = out.astype(o_ref.dtype)
        return carry

    lax.fori_loop(0, nC, chunk, 0)

    @pl.when(tstep == pl.num_programs(2) - 1)
    def _():
        for hh in hs:
            cout_ref[0, hh] = c_sc[hh, 0:dqk, :]
            nout_ref[0, hh] = n_sc[hh, :, 0:dqk]
            mout_ref[0, hh] = m_sc[hh]


def mlstm_mix(p, gates, b, t, b_gate, norm_g, c_all, n_all, m_all, layer, hb=4):
    heads, dqk, dv = c_all.shape[2], c_all.shape[3], c_all.shape[4]
    dqp = MLSTM_DQK_PAD
    dm = heads * dv
    L = math.gcd(t, MLSTM_CHUNK)
    tb = min(t, SEQ_BLOCK)
    nC, nc_step = t // L, tb // L
    nhb = heads // hb
    p3 = p.reshape(b, t, p.shape[1])
    g3 = gates.reshape(b, t, 2 * heads)
    ig = jnp.swapaxes(g3[..., :heads], 1, 2).reshape(b, heads, nC, L)
    fg = jnp.swapaxes(g3[..., heads:], 1, 2).reshape(b, heads, nC, L)
    q_col = 2 * dm + D_GATE + D_XATTN
    q_off, k_off = q_col // (hb * dqp), (q_col + heads * dqp) // (hb * dqp)
    seq = lambda width, off: pl.BlockSpec((1, tb, hb * width), lambda i, j, tt: (i, tt, off + j))
    gate_spec = pl.BlockSpec((1, hb, nc_step, L), lambda i, j, tt: (i, j, tt, 0))
    c_in = pl.BlockSpec((None, 1, hb, dqk, dv), lambda i, j, tt: (layer, i, j, 0, 0))
    n_in = pl.BlockSpec((None, 1, hb, 1, dqk), lambda i, j, tt: (layer, i, j, 0, 0))
    m_in = pl.BlockSpec((None, 1, hb, 1, 1), lambda i, j, tt: (layer, i, j, 0, 0))
    c_out = pl.BlockSpec((1, hb, dqk, dv), lambda i, j, tt: (i, j, 0, 0))
    n_out = pl.BlockSpec((1, hb, 1, dqk), lambda i, j, tt: (i, j, 0, 0))
    m_out = pl.BlockSpec((1, hb, 1, 1), lambda i, j, tt: (i, j, 0, 0))
    mix, c, n, m = pl.pallas_call(
        functools.partial(_mlstm_kernel, L=L, nC=nc_step, Hb=hb, heads=heads, dqk=dqk, dv=dv),
        grid=(b, nhb, t // tb),
        in_specs=[pl.BlockSpec(memory_space=pltpu.SMEM),
                  seq(dqp, q_off), seq(dqp, k_off), seq(dv, 0), seq(dv, nhb), seq(dv, 2 * nhb),
                  gate_spec, gate_spec,
                  pl.BlockSpec((1, hb * dv), lambda i, j, tt: (0, j)),
                  c_in, n_in, m_in],
        out_specs=[seq(dv, 0), c_out, n_out, m_out],
        out_shape=[jax.ShapeDtypeStruct((b, t, dm), BF16),
                   jax.ShapeDtypeStruct((b, heads, dqk, dv), F32),
                   jax.ShapeDtypeStruct((b, heads, 1, dqk), F32),
                   jax.ShapeDtypeStruct((b, heads, 1, 1), F32)],
        scratch_shapes=[pltpu.VMEM((hb, dqp, dv), F32), pltpu.VMEM((hb, 1, dqp), F32), pltpu.VMEM((hb, 1, 1), F32)],
        compiler_params=_cparams("parallel", "parallel", "arbitrary"),
        name="mlstm_mix",
    )(b_gate.astype(F32), p3, p3, p3, p3, p3, ig, fg, norm_g.reshape(1, dm).astype(F32),
      c_all, n_all.reshape(n_all.shape[:3] + (1, dqk)), m_all.reshape(m_all.shape[:3] + (1, 1)))
    return mix.reshape(b * t, dm), c[None], n.reshape(1, b, heads, dqk), m.reshape(1, b, heads)


def _seg_sum(x, lane_lo):
    s0 = jnp.sum(jnp.where(lane_lo, x, 0.0), axis=-1, keepdims=True)
    s1 = jnp.sum(jnp.where(lane_lo, 0.0, x), axis=-1, keepdims=True)
    return jnp.where(lane_lo, s0, s1)


def _rwkv_prep_kernel(k_ref, wlo_ref, alo_ref, w2_ref, a2_ref, w0_ref, a0_ref, kk_ref, ka_ref,
                      ld_ref, lc_ref, km_ref, av_ref, bv_ref, *, L):
    k = k_ref[...]
    lw = w0_ref[...] + _mm(jnp.tanh(wlo_ref[...]), w2_ref[...])
    log_w = -(jnp.maximum(-lw, 0.0) + jnp.log1p(jnp.exp(-jnp.abs(lw)))) - 0.5
    ld = -jnp.exp(log_w)
    ld_ref[...] = ld
    lc_ref[...] = _cumsum_rows(ld, L)
    a = _sigmoid(a0_ref[...] + _mm(alo_ref[...], a2_ref[...]))
    kk = k * kk_ref[...]
    tn = k.shape[1]
    lane_lo = lax.broadcasted_iota(jnp.int32, (k.shape[0], LANES), 1) < RWKV_HEAD_DIM
    for c in range(tn // LANES):
        cs = slice(c * LANES, (c + 1) * LANES)
        kc = kk[:, cs]
        kc = kc / jnp.maximum(jnp.sqrt(_seg_sum(kc * kc, lane_lo)), 1e-12)
        av_ref[:, cs] = -kc
        bv_ref[:, cs] = kc * a[:, cs]
    km_ref[...] = k * (1.0 + (a - 1.0) * ka_ref[...])


def rwkv_prep(p, k_col, lora, w2, a2, w0, a0, k_k, k_a, L, tm=512, tn=512):
    m = p.shape[0]
    dm = w2.shape[1]
    lp = RWKV_LORA_PAD
    tm = min(tm, m)
    assert tm % L == 0
    k_off = k_col // tn
    par = pl.BlockSpec((1, tn), lambda i, j: (0, j))
    out = pl.BlockSpec((tm, tn), lambda i, j: (i, j))
    row = lambda a: a.reshape(1, dm).astype(F32)
    return pl.pallas_call(
        functools.partial(_rwkv_prep_kernel, L=L),
        grid=(m // tm, dm // tn),
        in_specs=[pl.BlockSpec((tm, tn), lambda i, j: (i, k_off + j)),
                  pl.BlockSpec((tm, lp), lambda i, j: (i, 0)),
                  pl.BlockSpec((tm, lp), lambda i, j: (i, 1)),
                  pl.BlockSpec((lp, tn), lambda i, j: (0, j)),
                  pl.BlockSpec((lp, tn), lambda i, j: (0, j)),
                  par, par, par, par],
        out_specs=[out] * 5,
        out_shape=[jax.ShapeDtypeStruct((m, dm), F32)] * 5,
        compiler_params=_cparams("parallel", "parallel"),
        name="rwkv_prep",
    )(p, lora, lora, w2, a2, row(w0), row(a0), row(k_k), row(k_a))


def _rwkv_kernel(r_ref, ld_ref, lc_ref, k_ref, v_ref, a_ref, b_ref, gate_ref, rk_ref, lg_ref, lb_ref, s0_ref,
                 y_ref, sout_ref, sp_ref, *, L, Gp, Ns, nC):
    hd = RWKV_HEAD_DIM
    G = 2 * Gp
    R = G * L
    W = Gp * LANES
    tstep = pl.program_id(2)
    bm = (lax.broadcasted_iota(jnp.int32, (LANES, LANES), 0) // hd
          == lax.broadcasted_iota(jnp.int32, (LANES, LANES), 1) // hd)

    @pl.when(tstep == 0)
    def _():
        for p in range(Ns * Gp):
            x = s0_ref[0, p]
            sp_ref[p] = jnp.where(bm, jnp.concatenate([x, x], axis=1), 0.0)

    head_mask = (lax.broadcasted_iota(jnp.int32, (R, W), 0) // L
                 == lax.broadcasted_iota(jnp.int32, (R, W), 1) // hd)
    ri = lax.broadcasted_iota(jnp.int32, (R, R), 0)
    ci = lax.broadcasted_iota(jnp.int32, (R, R), 1)
    same = (ri // L) == (ci // L)
    strict = same & (ci < ri)
    ri2 = lax.broadcasted_iota(jnp.int32, (R, 2 * R), 0)
    ci2 = lax.broadcasted_iota(jnp.int32, (R, 2 * R), 1)
    ci2 = jnp.where(ci2 >= R, ci2 - R, ci2)
    incl2 = ((ri2 // L) == (ci2 // L)) & (ci2 <= ri2)
    lo = (lax.broadcasted_iota(jnp.int32, (L, LANES), 1) < hd)
    n_dbl = max(1, (L - 1).bit_length())
    stack = lambda x: jnp.where(head_mask, jnp.concatenate([x] * G, axis=0), 0.0).astype(BF16)

    streams = range(Ns)

    def chunk(c, carry):
        r0 = pl.multiple_of(c * L, L)
        ld_ = lambda ref, s: ref[0, pl.ds(r0, L), s * W:(s + 1) * W]
        ar2, bk2, v2, bkc, c_last = [], [], [], [], []
        for s in streams:
            r, ldec, lc, k, v, av, bv = (ld_(r_ref, s), ld_(ld_ref, s), ld_(lc_ref, s), ld_(k_ref, s),
                                         ld_(v_ref, s), ld_(a_ref, s), ld_(b_ref, s))
            lcl = lc[L - 1:L, :]
            e_neg = jnp.exp(-lc)
            e_rem = jnp.exp(lcl - lc)
            ar2.append(jnp.concatenate([stack(av * jnp.exp(lc - ldec)), stack(r * jnp.exp(lc))], axis=0))
            bk2.append(jnp.concatenate([stack(bv * e_neg), stack(k * e_neg)], axis=0))
            v2.append(stack(v))
            bkc.append(jnp.concatenate([stack(bv * e_rem), stack(k * e_rem)], axis=0))
            c_last.append(jnp.exp(lcl))
        g = [pl.dot(ar2[s], bk2[s], trans_b=True) for s in streams]
        nmat = [jnp.where(strict, g[s][:R, :R], 0.0).astype(BF16) for s in streams]
        a_ak = [jnp.where(strict, g[s][:R, R:], 0.0).astype(BF16) for s in streams]
        a_rbk = [jnp.where(incl2, g[s][R:, :], 0.0).astype(BF16) for s in streams]
        akv = [pl.dot(a_ak[s], v2[s]) for s in streams]
        minv = [nmat[s].astype(F32) for s in streams]
        for it in range(1, n_dbl):
            nmat = [pl.dot(nmat[s], nmat[s]).astype(BF16) for s in streams]
            minv = [minv[s] + nmat[s] + pl.dot(nmat[s], minv[s].astype(BF16)) for s in streams]
        ur0 = [jnp.concatenate(
            [pl.dot(ar2[s][:, p * LANES:(p + 1) * LANES], sp_ref[s * Gp + p].astype(BF16), trans_b=True)
             for p in range(Gp)], axis=1) for s in streams]
        x = [ur0[s][:R] + akv[s] for s in streams]
        x = [x[s] + pl.dot(minv[s].astype(BF16), x[s].astype(BF16)) for s in streams]
        xv = [jnp.concatenate([x[s].astype(BF16), v2[s]], axis=0) for s in streams]
        y2 = [ur0[s][R:] + pl.dot(a_rbk[s], xv[s]) for s in streams]
        for s in streams:
            y = y2[s][0:L]
            for j in range(1, G):
                y = y + y2[s][j * L:(j + 1) * L]
            for p in range(Gp):
                cs = slice(p * LANES, (p + 1) * LANES)
                gs = slice(s * W + p * LANES, s * W + (p + 1) * LANES)
                q = s * Gp + p
                sp_ref[q] = sp_ref[q] * c_last[s][:, cs] + pl.dot(xv[s][:, cs], bkc[s][:, cs], trans_a=True)
                yp = y[:, cs]
                rp, kp, vp = r_ref[0, pl.ds(r0, L), gs], k_ref[0, pl.ds(r0, L), gs], v_ref[0, pl.ds(r0, L), gs]
                mean = _seg_sum(yp, lo) * (1.0 / hd)
                yc = yp - mean
                var = _seg_sum(yc * yc, lo) * (1.0 / hd)
                yn = yc * lax.rsqrt(var + RWKV_GN_EPS) * lg_ref[:, gs] + lb_ref[:, gs]
                bonus = _seg_sum(rp * kp * rk_ref[:, gs], lo) * vp
                y_ref[0, pl.ds(r0, L), gs] = ((yn + bonus) * _silu(gate_ref[0, pl.ds(r0, L), gs])).astype(y_ref.dtype)
        return carry

    lax.fori_loop(0, nC, chunk, 0)

    @pl.when(tstep == pl.num_programs(2) - 1)
    def _():
        row_lo = lax.broadcasted_iota(jnp.int32, (LANES, hd), 0) < hd
        for p in range(Ns * Gp):
            sp = sp_ref[p]
            sout_ref[0, p] = jnp.where(row_lo, sp[:, :hd], sp[:, hd:])


def rwkv_mix(r_src, r_col, v_col, g_col, ld, lc, km, av, bv, b, t, L, r_k, ln_g, ln_b, s0):
    heads, hd = s0.shape[1], s0.shape[2]
    dm = heads * hd
    pairs = heads // 2
    gp = max(g for g in range(1, pairs + 1)
             if pairs % g == 0 and 2 * g * L <= RWKV_ROWS and (2 * g * L) % LANES == 0)
    ns = max(n for n in range(1, RWKV_STREAMS + 1) if (pairs // gp) % n == 0)
    tb = min(t, RWKV_SEQ_BLOCK)
    w = ns * gp * LANES
    assert r_col % w == 0 and v_col % w == 0 and g_col % w == 0
    r_off, v_off, g_off = r_col // w, v_col // w, g_col // w
    three = lambda a: a.reshape(b, t, a.shape[1])
    row = lambda a: a.reshape(1, dm).astype(F32)
    seq = lambda off: pl.BlockSpec((1, tb, w), lambda i, j, tt: (i, tt, off + j))
    par = pl.BlockSpec((1, w), lambda i, j, tt: (0, j))
    st_spec = pl.BlockSpec((1, ns * gp, LANES, hd), lambda i, j, tt: (i, j, 0, 0))
    s0p = s0.reshape(b, pairs, LANES, hd)
    y, s_out = pl.pallas_call(
        functools.partial(_rwkv_kernel, L=L, Gp=gp, Ns=ns, nC=tb // L),
        grid=(b, pairs // (gp * ns), t // tb),
        in_specs=[seq(r_off), seq(0), seq(0), seq(0), seq(v_off), seq(0), seq(0), seq(g_off), par, par, par, st_spec],
        out_specs=[seq(0), st_spec],
        out_shape=[jax.ShapeDtypeStruct((b, t, dm), BF16), jax.ShapeDtypeStruct(s0p.shape, F32)],
        scratch_shapes=[pltpu.VMEM((ns * gp, LANES, LANES), F32)],
        compiler_params=_cparams("parallel", "parallel", "arbitrary"),
        name="rwkv_mix",
    )(three(r_src), three(ld), three(lc), three(km), three(r_src), three(av), three(bv), three(r_src),
      row(r_k), row(ln_g), row(ln_b), s0p)
    return y.reshape(b * t, dm), s_out.reshape(s0.shape)


def _hgrn_lower_bounds(logits):
    p = jax.nn.softmax(logits.astype(F32), axis=0)
    return jnp.maximum(jnp.cumsum(p, axis=0) - p[0], 0.0)


def _mlstm_weights(w_in):
    dq = MLSTM_HEADS * MLSTM_DQK
    o = [0, dq, 2 * dq, 2 * dq + D_MIX, 2 * dq + D_MIX + MLSTM_HEADS, 2 * dq + D_MIX + 2 * MLSTM_HEADS]
    q, k, v = w_in[:, o[0]:o[1]], w_in[:, o[1]:o[2]], w_in[:, o[2]:o[3]]
    gates = w_in[:, o[3]:o[5]]
    rest = w_in[:, o[5]:]
    pad = lambda a: jnp.pad(a.reshape(-1, MLSTM_HEADS, MLSTM_DQK),
                            ((0, 0), (0, 0), (0, MLSTM_DQK_PAD - MLSTM_DQK))).reshape(-1, MLSTM_HEADS * MLSTM_DQK_PAD)
    main = jnp.concatenate([v, rest, pad(q), pad(k)], axis=1).astype(BF16)
    gates = jnp.pad(gates, ((0, 0), (0, LANES - 2 * MLSTM_HEADS))).astype(BF16)
    return main, gates


def _rwkv_weights(w_in, mu):
    c = [0]
    for n in (D_MIX, RWKV_LORA, D_MIX, D_MIX, RWKV_LORA, D_GATE, D_XATTN):
        c.append(c[-1] + n)
    sl = lambda i: w_in[:, c[i]:c[i + 1]]
    main = jnp.concatenate([sl(0), sl(2), sl(3), sl(5), sl(6)], axis=1).astype(BF16)
    padl = lambda a: jnp.pad(a, ((0, 0), (0, RWKV_LORA_PAD - RWKV_LORA)))
    lora = jnp.concatenate([padl(sl(1)), padl(sl(4))], axis=1).astype(BF16)
    mu_ext = jnp.concatenate([mu.astype(F32), jnp.zeros((1, D_MODEL), F32)], axis=0)
    tile = 1024
    main_groups = ([0] * (D_MIX // tile) + [2] * (D_MIX // tile) + [3] * (D_MIX // tile)
                   + [5] * (D_GATE // tile) + [6] * (D_XATTN // tile))
    return main, lora, mu_ext, main_groups, [1, 4]


def _trunk(x3, mk, mv, mem_cols, s_hgrn, s_mc, s_mn, s_mm, s_rwkv, s_shift, prm, hgrn_hb):
    b, t, d = x3.shape
    x = x3.reshape(b * t, d)
    norm_dtype = lambda i: F32 if i % N_MIXERS == 2 else BF16
    out = dict(hgrn=None, mc=[], mn=[], mm=[], rwkv=[], shift=[])
    h = rmsnorm(x, prm['norm_pre'][0], norm_dtype(0))
    for i in range(DEPTH):
        kind, j = i % N_MIXERS, i // N_MIXERS
        if kind == 0:
            p = matmul(h, prm['hgrn_w'], layer=j, tn=512)
            gate_col, q_col = 3 * D_MIX, 3 * D_MIX + D_GATE
            mix, out['hgrn'] = hgrn_mix(p, gate_col, b, t, prm['hgrn_lb'][j], prm['hgrn_norm'][j], s_hgrn, j, hgrn_hb,
                                        out['hgrn'])
        elif kind == 1:
            w_main, w_gates = prm['mlstm_w'][j]
            p = matmul(h, w_main)
            gates = matmul(h, w_gates, tn=LANES)[:, :2 * MLSTM_HEADS]
            gate_col, q_col = 2 * D_MIX, 2 * D_MIX + D_GATE
            mix, c, n, m = mlstm_mix(p, gates, b, t, prm['mlstm_b_gate'][j], prm['mlstm_norm'][j],
                                     s_mc, s_mn, s_mm, j)
            out['mc'].append(c)
            out['mn'].append(n)
            out['mm'].append(m)
        else:
            h3 = h.reshape(b, t, d)
            hp = jnp.concatenate([s_shift[j][:, None, :], h3[:, :-1]], axis=1).reshape(b * t, d)
            w_main, w_lora, mu_ext, main_groups, lora_groups = prm['rwkv_w'][j]
            mixes = token_shift_lerp(h, hp, mu_ext)
            p = matmul_grouped(mixes, w_main, main_groups)
            lora = matmul_grouped(mixes, w_lora, lora_groups, tn=RWKV_LORA_PAD)
            L = math.gcd(t, RWKV_CHUNK)
            ld, lc, km, av, bv = rwkv_prep(p, D_MIX, lora, prm['rwkv_w2'][j], prm['rwkv_a2'][j], prm['rwkv_w0'][j],
                                           prm['rwkv_a0'][j], prm['rwkv_k_k'][j], prm['rwkv_k_a'][j], L)
            gate_col, q_col = 3 * D_MIX, 3 * D_MIX + D_GATE
            mix, s = rwkv_mix(p, 0, 2 * D_MIX, gate_col, ld, lc, km, av, bv, b, t, L, prm['rwkv_r_k'][j],
                              prm['rwkv_ln_g'][j], prm['rwkv_ln_b'][j], s_rwkv[j])
            out['rwkv'].append(s)
            out['shift'].append(h3[:, -1])
        lyr, k_col, v_col = mem_cols(i)
        if isinstance(mk, list):
            xo = memattn(p, q_col, gate_col + D_MIX, mk[i], mv[i], lyr, k_col, v_col, b, t, X_HEADS)
        else:
            xo = memattn_cache(p, q_col, gate_col + D_MIX, mk, mv, lyr, b, t)
        if i + 1 < DEPTH:
            x, h = outproj(mix, xo, prm['w_out'], i, x, prm['norm_post'][i], prm['norm_pre'][i + 1], norm_dtype(i + 1))
        else:
            x, _ = outproj(mix, xo, prm['w_out'], i, x, prm['norm_post'][i])
    cat = lambda parts: parts[0] if len(parts) == 1 else jnp.concatenate(parts, axis=0)
    return (x.reshape(b, t, d), out['hgrn'], cat(out['mc']), cat(out['mn']), cat(out['mm']),
            jnp.stack(out['rwkv']), jnp.stack(out['shift']))


def kernel(x_prompt, x_sample, mem_prompt, cache_mem_k, cache_mem_v, state_hgrn, state_mlstm_c, state_mlstm_n, state_mlstm_m, state_rwkv, state_rwkv_shift, norm_pre, norm_post, norm_mem, w_mem_kv, w_out, hgrn_w_in, hgrn_lb_logits, hgrn_norm, mlstm_w_in, mlstm_b_gate, mlstm_norm, rwkv_w_in, rwkv_mu, rwkv_w0, rwkv_w2, rwkv_a0, rwkv_a2, rwkv_k_k, rwkv_k_a, rwkv_r_k, rwkv_ln_g, rwkv_ln_b):
    n_h, n_m, n_r = hgrn_w_in.shape[0], mlstm_w_in.shape[0], rwkv_w_in.shape[0]
    pad_lora = lambda a: jnp.pad(a, ((0, RWKV_LORA_PAD - RWKV_LORA), (0, 0))).astype(BF16)
    prm = dict(
        norm_pre=norm_pre, norm_post=norm_post, w_out=w_out.astype(BF16),
        hgrn_w=hgrn_w_in,
        hgrn_lb=_hgrn_lower_bounds(hgrn_lb_logits), hgrn_norm=hgrn_norm,
        mlstm_w=[_mlstm_weights(mlstm_w_in[j]) for j in range(n_m)],
        mlstm_b_gate=mlstm_b_gate, mlstm_norm=mlstm_norm,
        rwkv_w=[_rwkv_weights(rwkv_w_in[j], rwkv_mu[j]) for j in range(n_r)],
        rwkv_w0=rwkv_w0, rwkv_w2=[pad_lora(rwkv_w2[j]) for j in range(n_r)],
        rwkv_a0=rwkv_a0, rwkv_a2=[pad_lora(rwkv_a2[j]) for j in range(n_r)],
        rwkv_k_k=rwkv_k_k, rwkv_k_a=rwkv_k_a, rwkv_r_k=rwkv_r_k, rwkv_ln_g=rwkv_ln_g, rwkv_ln_b=rwkv_ln_b)

    bp, n_mem, d = mem_prompt.shape
    mem2 = mem_prompt.reshape(bp * n_mem, d)
    kvs = [matmul(rmsnorm(mem2, norm_mem[i], BF16), w_mem_kv[i].astype(BF16)).reshape(1, bp, n_mem, 2 * D_XATTN)
           for i in range(DEPTH)]
    heads_shape = (bp, n_mem, X_HEADS, X_HEAD_DIM)
    mem_k_prompt = jnp.stack([kv[0, :, :, :D_XATTN].reshape(heads_shape) for kv in kvs])
    mem_v_prompt = jnp.stack([kv[0, :, :, D_XATTN:].reshape(heads_shape) for kv in kvs])

    dt = x_prompt.dtype
    zeros = lambda *shape: jnp.zeros(shape, dt)
    (y_prompt, hgrn_p, mc_p, mn_p, mm_p, rwkv_p, shift_p) = _trunk(
        x_prompt, kvs, kvs, lambda i: (0, 0, D_XATTN),
        zeros(n_h, bp, HGRN_HEADS, HGRN_HEAD_DIM, HGRN_HEAD_DIM),
        zeros(n_m, bp, MLSTM_HEADS, MLSTM_DQK, MLSTM_DV),
        zeros(n_m, bp, MLSTM_HEADS, MLSTM_DQK),
        zeros(n_m, bp, MLSTM_HEADS),
        zeros(n_r, bp, RWKV_HEADS, RWKV_HEAD_DIM, RWKV_HEAD_DIM),
        zeros(n_r, bp, D_MODEL),
        prm, hgrn_hb=6)
    (y_sample, hgrn_s, mc_s, mn_s, mm_s, rwkv_s, shift_s) = _trunk(
        x_sample, cache_mem_k, cache_mem_v, lambda i: (i, 0, 0), state_hgrn, state_mlstm_c, state_mlstm_n, state_mlstm_m,
        state_rwkv, state_rwkv_shift, prm, hgrn_hb=8)
    return (y_prompt, y_sample, mem_k_prompt, mem_v_prompt,
            hgrn_p, mc_p, mn_p, mm_p, rwkv_p, shift_p,
            hgrn_s, mc_s, mn_s, mm_s, rwkv_s, shift_s)
```

```python
import functools
import math

import jax
import jax.numpy as jnp
from jax import lax
from jax.experimental import pallas as pl
from jax.experimental.pallas import tpu as pltpu

F32 = jnp.float32
BF16 = jnp.bfloat16

D_MODEL = 4096
DEPTH = 4
N_MIXERS = 3
D_MIX = 3 * D_MODEL // 4
D_XATTN = D_MODEL // 4
D_GATE = D_MIX + D_XATTN
N_MEM = 256
X_HEADS = 4
X_HEAD_DIM = D_XATTN // X_HEADS

HGRN_HEAD_DIM = 128
HGRN_HEADS = D_MIX // HGRN_HEAD_DIM
HGRN_CHUNK = 16

MLSTM_HEADS = 8
MLSTM_DV = D_MIX // MLSTM_HEADS
MLSTM_DQK = MLSTM_DV // 2
MLSTM_DQK_PAD = 256
MLSTM_CHUNK = 64

RWKV_HEAD_DIM = 64
RWKV_HEADS = D_MIX // RWKV_HEAD_DIM
RWKV_LORA = max(32, int(round(1.8 * math.sqrt(D_MIX) / 32)) * 32)
RWKV_LORA_PAD = 128
RWKV_GN_EPS = 64e-5
RWKV_CHUNK = 64
RWKV_ROWS = 256
RWKV_STREAMS = 3
RWKV_SEQ_BLOCK = 512

NORM_EPS = 1e-6
LOG2E = 1.4426950408889634
LANES = 128
SEQ_BLOCK = 512
HGRN_SEQ_BLOCK = 1024
VMEM_LIMIT = 48 * 1024 * 1024
VMEM_LIMIT_OUTPROJ = 56 * 1024 * 1024


def _cparams(*sem, vmem=VMEM_LIMIT):
    return pltpu.CompilerParams(dimension_semantics=sem, vmem_limit_bytes=vmem)


def _mm(a, b, **kw):
    return pl.dot(a.astype(BF16), b.astype(BF16), **kw)


def _sigmoid(x):
    return 1.0 / (1.0 + jnp.exp(-x))


def _silu(g):
    return g * _sigmoid(g)


def _log_sigmoid(x):
    return jnp.minimum(x, 0.0) - jnp.log1p(jnp.exp(-jnp.abs(x)))


def _cumsum_rows(x, block):
    row = lax.broadcasted_iota(jnp.int32, x.shape, 0) & (block - 1)
    sh = 1
    while sh < block:
        x = x + jnp.where(row >= sh, pltpu.roll(x, sh, axis=0), 0.0)
        sh *= 2
    return x


def _rmsnorm_kernel(x_ref, g_ref, o_ref):
    x = x_ref[...]
    ms = jnp.mean(x * x, axis=-1, keepdims=True)
    o_ref[...] = ((x * lax.rsqrt(ms + NORM_EPS)) * g_ref[...]).astype(o_ref.dtype)


def rmsnorm(x, gain, out_dtype, tm=256):
    m, d = x.shape
    tm = min(tm, m)
    return pl.pallas_call(
        _rmsnorm_kernel,
        grid=(m // tm,),
        in_specs=[pl.BlockSpec((tm, d), lambda i: (i, 0)), pl.BlockSpec((1, d), lambda i: (0, 0))],
        out_specs=pl.BlockSpec((tm, d), lambda i: (i, 0)),
        out_shape=jax.ShapeDtypeStruct((m, d), out_dtype),
        compiler_params=_cparams("parallel"),
        name="rmsnorm",
    )(x, gain.reshape(1, d))


def _mm_kernel(a_ref, w_ref, o_ref):
    o_ref[...] = _mm(a_ref[...], w_ref[...])


def matmul(a, w, layer=None, tm=1024, tn=1024):
    m, k = a.shape
    n = w.shape[-1]
    tm, tn = min(tm, m), min(tn, n)
    if layer is None:
        w_spec = pl.BlockSpec((k, tn), lambda i, j: (0, j))
    else:
        w_spec = pl.BlockSpec((None, k, tn), lambda i, j: (layer, 0, j))
    return pl.pallas_call(
        _mm_kernel,
        grid=(m // tm, n // tn),
        in_specs=[pl.BlockSpec((tm, k), lambda i, j: (i, 0)), w_spec],
        out_specs=pl.BlockSpec((tm, tn), lambda i, j: (i, j)),
        out_shape=jax.ShapeDtypeStruct((m, n), F32),
        compiler_params=_cparams("parallel", "parallel"),
        name="matmul",
    )(a, w)


def _lerp_kernel(h_ref, hp_ref, mu_ref, o_ref):
    h = h_ref[...]
    d = hp_ref[...] - h
    for g in range(o_ref.shape[0]):
        o_ref[g] = (h + d * mu_ref[g]).astype(o_ref.dtype)


def token_shift_lerp(h, hp, mu, tm=128):
    m, k = h.shape
    g = mu.shape[0]
    tm = min(tm, m)
    return pl.pallas_call(
        _lerp_kernel,
        grid=(m // tm,),
        in_specs=[pl.BlockSpec((tm, k), lambda i: (i, 0)),
                  pl.BlockSpec((tm, k), lambda i: (i, 0)),
                  pl.BlockSpec((g, 1, k), lambda i: (0, 0, 0))],
        out_specs=pl.BlockSpec((g, tm, k), lambda i: (0, i, 0)),
        out_shape=jax.ShapeDtypeStruct((g, m, k), BF16),
        compiler_params=_cparams("parallel"),
        name="token_shift_lerp",
    )(h, hp, mu.reshape(g, 1, k).astype(F32))


def matmul_grouped(a, w, tile_group, tm=1024, tn=1024):
    _, m, k = a.shape
    n = w.shape[1]
    tm, tn = min(tm, m), min(tn, n)
    assert len(tile_group) == n // tn and list(tile_group) == sorted(tile_group)

    def group(j):
        g = tile_group[0]
        for t in range(1, len(tile_group)):
            if tile_group[t] != tile_group[t - 1]:
                g = g + jnp.where(j >= t, tile_group[t] - tile_group[t - 1], 0)
        return g

    return pl.pallas_call(
        _mm_kernel,
        grid=(m // tm, n // tn),
        in_specs=[pl.BlockSpec((None, tm, k), lambda i, j: (group(j), i, 0)),
                  pl.BlockSpec((k, tn), lambda i, j: (0, j))],
        out_specs=pl.BlockSpec((tm, tn), lambda i, j: (i, j)),
        out_shape=jax.ShapeDtypeStruct((m, n), F32),
        compiler_params=_cparams("parallel", "parallel"),
        name="matmul_grouped",
    )(a, w)


def _outproj_kernel(mix_ref, xo_ref, w_ref, x_ref, gain_ref, ngain_ref, o_ref, *h_ref):
    d_mix = mix_ref.shape[1]
    y = pl.dot(mix_ref[...], w_ref[0:d_mix, :]) + pl.dot(xo_ref[...], w_ref[d_mix:, :])
    ms = jnp.mean(y * y, axis=-1, keepdims=True)
    x = x_ref[...] + (y * lax.rsqrt(ms + NORM_EPS)) * gain_ref[...]
    o_ref[...] = x
    if h_ref:
        ms = jnp.mean(x * x, axis=-1, keepdims=True)
        h_ref[0][...] = ((x * lax.rsqrt(ms + NORM_EPS)) * ngain_ref[...]).astype(h_ref[0].dtype)


def outproj(u_mix, u_xo, w_all, layer, x, gain, next_gain=None, next_dtype=None, tm=128):
    m, d_mix = u_mix.shape
    d_x = u_xo.shape[1]
    kdim, n = w_all.shape[1:]
    tm = min(tm, m)
    assert kdim == d_mix + d_x
    rows = pl.BlockSpec((tm, n), lambda i: (i, 0))
    vec = pl.BlockSpec((1, n), lambda i: (0, 0))
    fused = next_gain is not None
    out = pl.pallas_call(
        _outproj_kernel,
        grid=(m // tm,),
        in_specs=[pl.BlockSpec((tm, d_mix), lambda i: (i, 0)),
                  pl.BlockSpec((tm, d_x), lambda i: (i, 0)),
                  pl.BlockSpec((None, kdim, n), lambda i: (layer, 0, 0), pipeline_mode=pl.Buffered(1)),
                  rows, vec, vec],
        out_specs=[rows, rows] if fused else [rows],
        out_shape=[jax.ShapeDtypeStruct((m, n), F32)] + ([jax.ShapeDtypeStruct((m, n), next_dtype)] if fused else []),
        compiler_params=_cparams("parallel", vmem=VMEM_LIMIT_OUTPROJ),
        name="outproj",
    )(u_mix, u_xo, w_all, x, gain.reshape(1, n), (next_gain if fused else gain).reshape(1, n))
    return (out[0], out[1]) if fused else (out[0], None)


def _memattn_kernel(q_ref, g_ref, k_ref, v_ref, o_ref, *, heads, hd, bb):
    scale = hd ** -0.5
    for i in range(bb):
        for h in range(heads):
            cs = slice(h * hd, (h + 1) * hd)
            s = _mm(q_ref[i, :, cs], k_ref[i, :, cs], trans_b=True) * scale
            s = s - jnp.max(s, axis=-1, keepdims=True)
            e = jnp.exp(s)
            p = e / jnp.sum(e, axis=-1, keepdims=True)
            o_ref[i, :, cs] = (_mm(p, v_ref[i, :, cs]) * _silu(g_ref[i, :, cs])).astype(o_ref.dtype)


def memattn(p, q_col, g_col, mk, mv, layer, k_col, v_col, b, t, heads, tq=512, batch_block=4):
    n_mem = mk.shape[2]
    w = D_XATTN
    hd = w // heads
    tq = min(tq, t)
    bb = batch_block if tq == t and b % batch_block == 0 else 1
    assert q_col % w == 0 and g_col % w == 0 and k_col % w == 0 and v_col % w == 0
    q_off, g_off, k_off, v_off = q_col // w, g_col // w, k_col // w, v_col // w
    p3 = p.reshape(b, t, p.shape[1])
    out = pl.pallas_call(
        functools.partial(_memattn_kernel, heads=heads, hd=hd, bb=bb),
        grid=(b // bb, t // tq),
        in_specs=[pl.BlockSpec((bb, tq, w), lambda i, j: (i, j, q_off)),
                  pl.BlockSpec((bb, tq, w), lambda i, j: (i, j, g_off)),
                  pl.BlockSpec((None, bb, n_mem, w), lambda i, j: (layer, i, 0, k_off)),
                  pl.BlockSpec((None, bb, n_mem, w), lambda i, j: (layer, i, 0, v_off))],
        out_specs=pl.BlockSpec((bb, tq, w), lambda i, j: (i, j, 0)),
        out_shape=jax.ShapeDtypeStruct((b, t, w), BF16),
        compiler_params=_cparams("parallel", "parallel"),
        name="memattn",
    )(p3, p3, mk, mv)
    return out.reshape(b * t, w)


def _memattn_cache_kernel(q_ref, g_ref, k_ref, v_ref, o_ref, *, heads, rows, bb):
    n_col = k_ref.shape[1]
    scale = (2 * LANES) ** -0.5
    col = lax.broadcasted_iota(jnp.int32, (rows, n_col), 1)
    same_head = (col % heads) == (lax.broadcasted_iota(jnp.int32, (rows, n_col), 0) % heads)
    half = (col // heads) % 2
    keep = [same_head & (half == 0), same_head & (half == 1)]
    for i in range(bb):
        k = k_ref[i].astype(BF16)
        v = v_ref[i].astype(BF16)
        s = [jnp.where(keep[c], _mm(q_ref[i, c], k, trans_b=True), 0.0) for c in range(2)]
        s = s[0] + s[1]
        s = s + pltpu.roll(s, n_col - heads, axis=1)
        s = jnp.where(keep[0], s * scale, -jnp.inf)
        e = jnp.exp(s - jnp.max(s, axis=-1, keepdims=True))
        p = e / jnp.sum(e, axis=-1, keepdims=True)
        for c in range(2):
            pc = p if c == 0 else pltpu.roll(p, heads, axis=1)
            o_ref[i, c] = (pl.dot(pc.astype(BF16), v) * _silu(g_ref[i, c])).astype(o_ref.dtype)


def memattn_cache(p, q_col, g_col, cache_k, cache_v, layer, b, t, batch_block=4):
    heads = cache_k.shape[3]
    assert cache_k.shape[4] == 2 * LANES
    n_mem = cache_k.shape[2]
    rows, w = t * heads, heads * 2 * LANES
    bb = batch_block if b % batch_block == 0 else 1

    def rows_view(a):
        a = a.reshape(a.shape[0], b, n_mem, heads, 2, LANES).transpose(0, 1, 2, 4, 3, 5)
        return a.reshape(a.shape[0] * b, n_mem * 2 * heads, LANES)

    def halves(a):
        return a.reshape(b, t, heads, 2, LANES).transpose(0, 3, 1, 2, 4).reshape(b, 2, rows, LANES)

    tok = pl.BlockSpec((bb, 2, rows, LANES), lambda i: (i, 0, 0, 0))
    mem = pl.BlockSpec((bb, n_mem * 2 * heads, LANES), lambda i: (layer * (b // bb) + i, 0, 0))
    out = pl.pallas_call(
        functools.partial(_memattn_cache_kernel, heads=heads, rows=rows, bb=bb),
        grid=(b // bb,),
        in_specs=[tok, tok, mem, mem],
        out_specs=tok,
        out_shape=jax.ShapeDtypeStruct((b, 2, rows, LANES), BF16),
        compiler_params=_cparams("parallel"),
        name="memattn_cache",
    )(halves(p[:, q_col:q_col + w]), halves(p[:, g_col:g_col + w]), rows_view(cache_k), rows_view(cache_v))
    return out.reshape(b, 2, t, heads, LANES).transpose(0, 2, 3, 1, 4).reshape(b * t, w)


def _hgrn_kernel(q_ref, f_ref, i_ref, gate_ref, loglb_ref, l1m_ref, oml_ref, g_ref, s0_ref, *rest, L, Hb, nC):
    o_ref, sout_ref, st_ref = rest[-3:]
    hd = HGRN_HEAD_DIM
    tstep = pl.program_id(2)

    @pl.when(tstep == 0)
    def _():
        for hh in range(Hb):
            st_ref[hh] = s0_ref[0, hh].T

    sub = 8
    row = lax.broadcasted_iota(jnp.int32, (sub, hd), 0)
    lane = lax.broadcasted_iota(jnp.int32, (sub, hd), 1)

    def gates(c):
        r0 = pl.multiple_of(c * L, L)
        staged = []
        for hh in range(Hb):
            cs = slice(hh * hd, (hh + 1) * hd)
            q = q_ref[0, pl.ds(r0, L), cs]
            f = f_ref[0, pl.ds(r0, L), cs]
            vh = i_ref[0, pl.ds(r0, L), cs]
            qh = q * _sigmoid(q)
            a = loglb_ref[:, cs]
            cc = l1m_ref[:, cs] + _log_sigmoid(f)
            lf = jnp.maximum(a, cc) + jnp.log1p(jnp.exp(-jnp.abs(a - cc)))
            kh = oml_ref[:, cs] * _sigmoid(-f)
            b = _cumsum_rows(lf, L) * LOG2E
            pieces = []
            for r0p in range(0, L, sub):
                bp, qp = b[r0p:r0p + sub, :], qh[r0p:r0p + sub, :]
                attn = jnp.zeros((sub, hd), F32)
                for s in range(min(L, r0p + sub)):
                    dec = jnp.exp2(bp - b[s:s + 1, :])
                    if s > r0p:
                        dec = jnp.where(row >= s - r0p, dec, 0.0)
                    col = jnp.sum((qp * kh[s:s + 1, :]) * dec, axis=-1, keepdims=True)
                    attn = jnp.where(lane == s, col, attn)
                pieces.append(attn)
            attn = pieces[0] if len(pieces) == 1 else jnp.concatenate(pieces, axis=0)
            b_last = b[L - 1:L, :]
            staged.append(((qh * jnp.exp2(b)).astype(BF16), attn[:, :L].astype(BF16), vh.astype(BF16),
                           (kh * jnp.exp2(b_last - b)).astype(BF16), jnp.exp2(b_last)))
        return tuple(staged)

    def update(c, staged):
        r0 = pl.multiple_of(c * L, L)
        for hh in range(Hb):
            cs = slice(hh * hd, (hh + 1) * hd)
            qe, attn, vh, kd, e_last = staged[hh]
            st = st_ref[hh]
            o = pl.dot(qe, st.astype(BF16), trans_b=True) + pl.dot(attn, vh)
            o = o * lax.rsqrt(jnp.mean(o * o, axis=-1, keepdims=True) + NORM_EPS)
            o_ref[0, pl.ds(r0, L), cs] = ((o * g_ref[:, cs]) * _silu(gate_ref[0, pl.ds(r0, L), cs])).astype(o_ref.dtype)
            st_ref[hh] = st * e_last + pl.dot(vh, kd, trans_a=True)

    def body(c, staged):
        nxt = gates(c + 1)
        update(c, staged)
        return nxt

    update(nC - 1, lax.fori_loop(0, nC - 1, body, gates(0)))

    @pl.when(tstep == pl.num_programs(2) - 1)
    def _():
        for hh in range(Hb):
            sout_ref[0, hh] = st_ref[hh].T


def hgrn_mix(p, gate_col, b, t, lb, norm_g, s_all, layer, hb, s_new=None):
    heads, hd = s_all.shape[2], s_all.shape[3]
    dm = heads * hd
    L = math.gcd(t, HGRN_CHUNK)
    tb = min(t, HGRN_SEQ_BLOCK)
    w = hb * hd
    nhb = heads // hb
    assert gate_col % w == 0
    p3 = p.reshape(b, t, p.shape[1])
    row = lambda a: a.reshape(1, dm).astype(F32)
    par_spec = pl.BlockSpec((1, w), lambda i, j, tt: (0, j))
    st_spec = pl.BlockSpec((None, 1, hb, hd, hd), lambda i, j, tt: (layer, i, j, 0, 0))
    args = [p3, p3, p3, p3, row(jnp.log(lb)), row(jnp.log1p(-lb)), row(1.0 - lb), row(norm_g), s_all]
    in_specs = [pl.BlockSpec((1, tb, w), lambda i, j, tt: (i, tt, j)),
                pl.BlockSpec((1, tb, w), lambda i, j, tt: (i, tt, nhb + j)),
                pl.BlockSpec((1, tb, w), lambda i, j, tt: (i, tt, 2 * nhb + j)),
                pl.BlockSpec((1, tb, w), lambda i, j, tt: (i, tt, gate_col // w + j)),
                par_spec, par_spec, par_spec, par_spec, st_spec]
    aliases = {}
    if s_new is not None:
        args.append(s_new)
        in_specs.append(pl.BlockSpec(memory_space=pl.ANY))
        aliases = {len(args) - 1: 1}
    mix, s_out = pl.pallas_call(
        functools.partial(_hgrn_kernel, L=L, Hb=hb, nC=tb // L),
        grid=(b, nhb, t // tb),
        in_specs=in_specs,
        out_specs=[pl.BlockSpec((1, tb, w), lambda i, j, tt: (i, tt, j)), st_spec],
        out_shape=[jax.ShapeDtypeStruct((b, t, dm), BF16), jax.ShapeDtypeStruct(s_all.shape, F32)],
        scratch_shapes=[pltpu.VMEM((hb, hd, hd), F32)],
        input_output_aliases=aliases,
        compiler_params=_cparams("parallel", "parallel", "arbitrary"),
        name="hgrn_mix",
    )(*args)
    return mix.reshape(b * t, dm), s_out


def _mlstm_kernel(bg_ref, q_ref, k_ref, v_ref, op_ref, gate_ref, ig_ref, fg_ref, g_ref, c0_ref, n0_ref, m0_ref,
                  o_ref, cout_ref, nout_ref, mout_ref, c_sc, n_sc, m_sc, *, L, nC, Hb, heads, dqk, dv):
    dqp = MLSTM_DQK_PAD
    tstep = pl.program_id(2)
    h0 = pl.program_id(1) * Hb
    hs = range(Hb)

    @pl.when(tstep == 0)
    def _():
        c_sc[...] = jnp.zeros_like(c_sc)
        n_sc[...] = jnp.zeros_like(n_sc)
        for hh in hs:
            c_sc[hh, 0:dqk, :] = c0_ref[0, hh]
            n_sc[hh, :, 0:dqk] = n0_ref[0, hh]
            m_sc[hh] = m0_ref[0, hh]

    ti = lax.broadcasted_iota(jnp.int32, (L, L), 0)
    si = lax.broadcasted_iota(jnp.int32, (L, L), 1)
    eye, low = ti == si, si <= ti
    col = lambda r: jnp.sum(jnp.where(eye, r, 0.0), axis=1, keepdims=True)
    kscale = dqk ** -0.5

    def chunk(c, carry):
        r0 = pl.multiple_of(c * L, L)
        q = [q_ref[0, pl.ds(r0, L), hh * dqp:(hh + 1) * dqp] for hh in hs]
        k = [k_ref[0, pl.ds(r0, L), hh * dqp:(hh + 1) * dqp] * kscale for hh in hs]
        v = [v_ref[0, pl.ds(r0, L), hh * dv:(hh + 1) * dv] for hh in hs]
        qb = [x.astype(BF16) for x in q]
        vb = [x.astype(BF16) for x in v]
        s_qk = [pl.dot(qb[hh], k[hh].astype(BF16), trans_b=True) for hh in hs]
        q_c = [pl.dot(qb[hh], c_sc[hh].astype(BF16)) for hh in hs]
        w_inter, a, m_t, b_col, ic_col, m_prev = [], [], [], [], [], []
        for hh in hs:
            ic_row = ig_ref[0, hh, pl.ds(c, 1), :] + bg_ref[h0 + hh]
            lf_row = _log_sigmoid(fg_ref[0, hh, pl.ds(c, 1), :] + bg_ref[heads + h0 + hh])
            bc = jnp.sum(jnp.where(low, lf_row, 0.0), axis=1, keepdims=True)
            b_row = jnp.sum(jnp.where(ti <= si, col(lf_row), 0.0), axis=0, keepdims=True)
            log_w = jnp.where(low, bc - b_row + ic_row, -jnp.inf)
            m = m_sc[hh]
            log_inter = bc + m
            mt = jnp.maximum(log_inter, jnp.max(log_w, axis=1, keepdims=True))
            w_inter.append(jnp.exp(log_inter - mt))
            a.append(jnp.exp(log_w - mt) * s_qk[hh])
            m_t.append(mt)
            b_col.append(bc)
            ic_col.append(col(ic_row))
            m_prev.append(m)
        a_v = [pl.dot(a[hh].astype(BF16), vb[hh]) for hh in hs]
        kw, decay = [], []
        for hh in hs:
            m_last = m_t[hh][L - 1:L, :]
            b_last = b_col[hh][L - 1:L, :]
            kw.append(jnp.exp(b_last - b_col[hh] + ic_col[hh] - m_last) * k[hh])
            decay.append(jnp.exp(b_last + m_prev[hh] - m_last))
            m_sc[hh] = m_last
        kw_v = [pl.dot(kw[hh].astype(BF16), vb[hh], trans_a=True) for hh in hs]
        for hh in hs:
            n = n_sc[hh]
            num = w_inter[hh] * q_c[hh] + a_v[hh]
            den = w_inter[hh] * jnp.sum(q[hh] * n, axis=1, keepdims=True) + jnp.sum(a[hh], axis=1, keepdims=True)
            hid = num / jnp.maximum(jnp.abs(den), jnp.exp(-m_t[hh]))
            c_sc[hh] = decay[hh] * c_sc[hh] + kw_v[hh]
            n_sc[hh] = decay[hh] * n + jnp.sum(kw[hh], axis=0, keepdims=True)
            vs = slice(hh * dv, (hh + 1) * dv)
            hn = hid * lax.rsqrt(jnp.mean(hid * hid, axis=1, keepdims=True) + NORM_EPS) * g_ref[:, vs]
            out = (_sigmoid(op_ref[0, pl.ds(r0, L), vs]) * hn) * _silu(gate_ref[0, pl.ds(r0, L), vs])
            o_ref[0, pl.ds(r0, L), vs] = out.astype(o_ref.dtype)
        return carry

    lax.fori_loop(0, nC, chunk, 0)

    @pl.when(tstep == pl.num_programs(2) - 1)
    def _():
        for hh in hs:
            cout_ref[0, hh] = c_sc[hh, 0:dqk, :]
            nout_ref[0, hh] = n_sc[hh, :, 0:dqk]
            mout_ref[0, hh] = m_sc[hh]


def mlstm_mix(p, gates, b, t, b_gate, norm_g, c_all, n_all, m_all, layer, hb=4):
    heads, dqk, dv = c_all.shape[2], c_all.shape[3], c_all.shape[4]
    dqp = MLSTM_DQK_PAD
    dm = heads * dv
    L = math.gcd(t, MLSTM_CHUNK)
    tb = min(t, SEQ_BLOCK)
    nC, nc_step = t // L, tb // L
    nhb = heads // hb
    p3 = p.reshape(b, t, p.shape[1])
    g3 = gates.reshape(b, t, 2 * heads)
    ig = jnp.swapaxes(g3[..., :heads], 1, 2).reshape(b, heads, nC, L)
    fg = jnp.swapaxes(g3[..., heads:], 1, 2).reshape(b, heads, nC, L)
    q_col = 2 * dm + D_GATE + D_XATTN
    q_off, k_off = q_col // (hb * dqp), (q_col + heads * dqp) // (hb * dqp)
    seq = lambda width, off: pl.BlockSpec((1, tb, hb * width), lambda i, j, tt: (i, tt, off + j))
    gate_spec = pl.BlockSpec((1, hb, nc_step, L), lambda i, j, tt: (i, j, tt, 0))
    c_in = pl.BlockSpec((None, 1, hb, dqk, dv), lambda i, j, tt: (layer, i, j, 0, 0))
    n_in = pl.BlockSpec((None, 1, hb, 1, dqk), lambda i, j, tt: (layer, i, j, 0, 0))
    m_in = pl.BlockSpec((None, 1, hb, 1, 1), lambda i, j, tt: (layer, i, j, 0, 0))
    c_out = pl.BlockSpec((1, hb, dqk, dv), lambda i, j, tt: (i, j, 0, 0))
    n_out = pl.BlockSpec((1, hb, 1, dqk), lambda i, j, tt: (i, j, 0, 0))
    m_out = pl.BlockSpec((1, hb, 1, 1), lambda i, j, tt: (i, j, 0, 0))
    mix, c, n, m = pl.pallas_call(
        functools.partial(_mlstm_kernel, L=L, nC=nc_step, Hb=hb, heads=heads, dqk=dqk, dv=dv),
        grid=(b, nhb, t // tb),
        in_specs=[pl.BlockSpec(memory_space=pltpu.SMEM),
                  seq(dqp, q_off), seq(dqp, k_off), seq(dv, 0), seq(dv, nhb), seq(dv, 2 * nhb),
                  gate_spec, gate_spec,
                  pl.BlockSpec((1, hb * dv), lambda i, j, tt: (0, j)),
                  c_in, n_in, m_in],
        out_specs=[seq(dv, 0), c_out, n_out, m_out],
        out_shape=[jax.ShapeDtypeStruct((b, t, dm), BF16),
                   jax.ShapeDtypeStruct((b, heads, dqk, dv), F32),
                   jax.ShapeDtypeStruct((b, heads, 1, dqk), F32),
                   jax.ShapeDtypeStruct((b, heads, 1, 1), F32)],
        scratch_shapes=[pltpu.VMEM((hb, dqp, dv), F32), pltpu.VMEM((hb, 1, dqp), F32), pltpu.VMEM((hb, 1, 1), F32)],
        compiler_params=_cparams("parallel", "parallel", "arbitrary"),
        name="mlstm_mix",
    )(b_gate.astype(F32), p3, p3, p3, p3, p3, ig, fg, norm_g.reshape(1, dm).astype(F32),
      c_all, n_all.reshape(n_all.shape[:3] + (1, dqk)), m_all.reshape(m_all.shape[:3] + (1, 1)))
    return mix.reshape(b * t, dm), c[None], n.reshape(1, b, heads, dqk), m.reshape(1, b, heads)


def _seg_sum(x, lane_lo):
    s0 = jnp.sum(jnp.where(lane_lo, x, 0.0), axis=-1, keepdims=True)
    s1 = jnp.sum(jnp.where(lane_lo, 0.0, x), axis=-1, keepdims=True)
    return jnp.where(lane_lo, s0, s1)


def _rwkv_prep_kernel(k_ref, wlo_ref, alo_ref, w2_ref, a2_ref, w0_ref, a0_ref, kk_ref, ka_ref,
                      ld_ref, lc_ref, km_ref, av_ref, bv_ref, *, L):
    k = k_ref[...]
    lw = w0_ref[...] + _mm(jnp.tanh(wlo_ref[...]), w2_ref[...])
    log_w = -(jnp.maximum(-lw, 0.0) + jnp.log1p(jnp.exp(-jnp.abs(lw)))) - 0.5
    ld = -jnp.exp(log_w)
    ld_ref[...] = ld
    lc_ref[...] = _cumsum_rows(ld, L)
    a = _sigmoid(a0_ref[...] + _mm(alo_ref[...], a2_ref[...]))
    kk = k * kk_ref[...]
    tn = k.shape[1]
    lane_lo = lax.broadcasted_iota(jnp.int32, (k.shape[0], LANES), 1) < RWKV_HEAD_DIM
    for c in range(tn // LANES):
        cs = slice(c * LANES, (c + 1) * LANES)
        kc = kk[:, cs]
        kc = kc / jnp.maximum(jnp.sqrt(_seg_sum(kc * kc, lane_lo)), 1e-12)
        av_ref[:, cs] = -kc
        bv_ref[:, cs] = kc * a[:, cs]
    km_ref[...] = k * (1.0 + (a - 1.0) * ka_ref[...])


def rwkv_prep(p, k_col, lora, w2, a2, w0, a0, k_k, k_a, L, tm=512, tn=512):
    m = p.shape[0]
    dm = w2.shape[1]
    lp = RWKV_LORA_PAD
    tm = min(tm, m)
    assert tm % L == 0
    k_off = k_col // tn
    par = pl.BlockSpec((1, tn), lambda i, j: (0, j))
    out = pl.BlockSpec((tm, tn), lambda i, j: (i, j))
    row = lambda a: a.reshape(1, dm).astype(F32)
    return pl.pallas_call(
        functools.partial(_rwkv_prep_kernel, L=L),
        grid=(m // tm, dm // tn),
        in_specs=[pl.BlockSpec((tm, tn), lambda i, j: (i, k_off + j)),
                  pl.BlockSpec((tm, lp), lambda i, j: (i, 0)),
                  pl.BlockSpec((tm, lp), lambda i, j: (i, 1)),
                  pl.BlockSpec((lp, tn), lambda i, j: (0, j)),
                  pl.BlockSpec((lp, tn), lambda i, j: (0, j)),
                  par, par, par, par],
        out_specs=[out] * 5,
        out_shape=[jax.ShapeDtypeStruct((m, dm), F32)] * 5,
        compiler_params=_cparams("parallel", "parallel"),
        name="rwkv_prep",
    )(p, lora, lora, w2, a2, row(w0), row(a0), row(k_k), row(k_a))


def _rwkv_kernel(r_ref, ld_ref, lc_ref, k_ref, v_ref, a_ref, b_ref, gate_ref, rk_ref, lg_ref, lb_ref, s0_ref,
                 y_ref, sout_ref, sp_ref, *, L, Gp, Ns, nC):
    hd = RWKV_HEAD_DIM
    G = 2 * Gp
    R = G * L
    W = Gp * LANES
    tstep = pl.program_id(2)
    bm = (lax.broadcasted_iota(jnp.int32, (LANES, LANES), 0) // hd
          == lax.broadcasted_iota(jnp.int32, (LANES, LANES), 1) // hd)

    @pl.when(tstep == 0)
    def _():
        for p in range(Ns * Gp):
            x = s0_ref[0, p]
            sp_ref[p] = jnp.where(bm, jnp.concatenate([x, x], axis=1), 0.0)

    head_mask = (lax.broadcasted_iota(jnp.int32, (R, W), 0) // L
                 == lax.broadcasted_iota(jnp.int32, (R, W), 1) // hd)
    ri = lax.broadcasted_iota(jnp.int32, (R, R), 0)
    ci = lax.broadcasted_iota(jnp.int32, (R, R), 1)
    same = (ri // L) == (ci // L)
    strict = same & (ci < ri)
    ri2 = lax.broadcasted_iota(jnp.int32, (R, 2 * R), 0)
    ci2 = lax.broadcasted_iota(jnp.int32, (R, 2 * R), 1)
    ci2 = jnp.where(ci2 >= R, ci2 - R, ci2)
    incl2 = ((ri2 // L) == (ci2 // L)) & (ci2 <= ri2)
    lo = (lax.broadcasted_iota(jnp.int32, (L, LANES), 1) < hd)
    n_dbl = max(1, (L - 1).bit_length())
    stack = lambda x: jnp.where(head_mask, jnp.concatenate([x] * G, axis=0), 0.0).astype(BF16)

    streams = range(Ns)

    def chunk(c, carry):
        r0 = pl.multiple_of(c * L, L)
        ld_ = lambda ref, s: ref[0, pl.ds(r0, L), s * W:(s + 1) * W]
        ar2, bk2, v2, bkc, c_last = [], [], [], [], []
        for s in streams:
            r, ldec, lc, k, v, av, bv = (ld_(r_ref, s), ld_(ld_ref, s), ld_(lc_ref, s), ld_(k_ref, s),
                                         ld_(v_ref, s), ld_(a_ref, s), ld_(b_ref, s))
            lcl = lc[L - 1:L, :]
            e_neg = jnp.exp(-lc)
            e_rem = jnp.exp(lcl - lc)
            ar2.append(jnp.concatenate([stack(av * jnp.exp(lc - ldec)), stack(r * jnp.exp(lc))], axis=0))
            bk2.append(jnp.concatenate([stack(bv * e_neg), stack(k * e_neg)], axis=0))
            v2.append(stack(v))
            bkc.append(jnp.concatenate([stack(bv * e_rem), stack(k * e_rem)], axis=0))
            c_last.append(jnp.exp(lcl))
        g = [pl.dot(ar2[s], bk2[s], trans_b=True) for s in streams]
        nmat = [jnp.where(strict, g[s][:R, :R], 0.0).astype(BF16) for s in streams]
        a_ak = [jnp.where(strict, g[s][:R, R:], 0.0).astype(BF16) for s in streams]
        a_rbk = [jnp.where(incl2, g[s][R:, :], 0.0).astype(BF16) for s in streams]
        akv = [pl.dot(a_ak[s], v2[s]) for s in streams]
        minv = [nmat[s].astype(F32) for s in streams]
        for it in range(1, n_dbl):
            nmat = [pl.dot(nmat[s], nmat[s]).astype(BF16) for s in streams]
            minv = [minv[s] + nmat[s] + pl.dot(nmat[s], minv[s].astype(BF16)) for s in streams]
        ur0 = [jnp.concatenate(
            [pl.dot(ar2[s][:, p * LANES:(p + 1) * LANES], sp_ref[s * Gp + p].astype(BF16), trans_b=True)
             for p in range(Gp)], axis=1) for s in streams]
        x = [ur0[s][:R] + akv[s] for s in streams]
        x = [x[s] + pl.dot(minv[s].astype(BF16), x[s].astype(BF16)) for s in streams]
        xv = [jnp.concatenate([x[s].astype(BF16), v2[s]], axis=0) for s in streams]
        y2 = [ur0[s][R:] + pl.dot(a_rbk[s], xv[s]) for s in streams]
        for s in streams:
            y = y2[s][0:L]
            for j in range(1, G):
                y = y + y2[s][j * L:(j + 1) * L]
            for p in range(Gp):
                cs = slice(p * LANES, (p + 1) * LANES)
                gs = slice(s * W + p * LANES, s * W + (p + 1) * LANES)
                q = s * Gp + p
                sp_ref[q] = sp_ref[q] * c_last[s][:, cs] + pl.dot(xv[s][:, cs], bkc[s][:, cs], trans_a=True)
                yp = y[:, cs]
                rp, kp, vp = r_ref[0, pl.ds(r0, L), gs], k_ref[0, pl.ds(r0, L), gs], v_ref[0, pl.ds(r0, L), gs]
                mean = _seg_sum(yp, lo) * (1.0 / hd)
                yc = yp - mean
                var = _seg_sum(yc * yc, lo) * (1.0 / hd)
                yn = yc * lax.rsqrt(var + RWKV_GN_EPS) * lg_ref[:, gs] + lb_ref[:, gs]
                bonus = _seg_sum(rp * kp * rk_ref[:, gs], lo) * vp
                y_ref[0, pl.ds(r0, L), gs] = ((yn + bonus) * _silu(gate_ref[0, pl.ds(r0, L), gs])).astype(y_ref.dtype)
        return carry

    lax.fori_loop(0, nC, chunk, 0)

    @pl.when(tstep == pl.num_programs(2) - 1)
    def _():
        row_lo = lax.broadcasted_iota(jnp.int32, (LANES, hd), 0) < hd
        for p in range(Ns * Gp):
            sp = sp_ref[p]
            sout_ref[0, p] = jnp.where(row_lo, sp[:, :hd], sp[:, hd:])


def rwkv_mix(r_src, r_col, v_col, g_col, ld, lc, km, av, bv, b, t, L, r_k, ln_g, ln_b, s0):
    heads, hd = s0.shape[1], s0.shape[2]
    dm = heads * hd
    pairs = heads // 2
    gp = max(g for g in range(1, pairs + 1)
             if pairs % g == 0 and 2 * g * L <= RWKV_ROWS and (2 * g * L) % LANES == 0)
    ns = max(n for n in range(1, RWKV_STREAMS + 1) if (pairs // gp) % n == 0)
    tb = min(t, RWKV_SEQ_BLOCK)
    w = ns * gp * LANES
    assert r_col % w == 0 and v_col % w == 0 and g_col % w == 0
    r_off, v_off, g_off = r_col // w, v_col // w, g_col // w
    three = lambda a: a.reshape(b, t, a.shape[1])
    row = lambda a: a.reshape(1, dm).astype(F32)
    seq = lambda off: pl.BlockSpec((1, tb, w), lambda i, j, tt: (i, tt, off + j))
    par = pl.BlockSpec((1, w), lambda i, j, tt: (0, j))
    st_spec = pl.BlockSpec((1, ns * gp, LANES, hd), lambda i, j, tt: (i, j, 0, 0))
    s0p = s0.reshape(b, pairs, LANES, hd)
    y, s_out = pl.pallas_call(
        functools.partial(_rwkv_kernel, L=L, Gp=gp, Ns=ns, nC=tb // L),
        grid=(b, pairs // (gp * ns), t // tb),
        in_specs=[seq(r_off), seq(0), seq(0), seq(0), seq(v_off), seq(0), seq(0), seq(g_off), par, par, par, st_spec],
        out_specs=[seq(0), st_spec],
        out_shape=[jax.ShapeDtypeStruct((b, t, dm), BF16), jax.ShapeDtypeStruct(s0p.shape, F32)],
        scratch_shapes=[pltpu.VMEM((ns * gp, LANES, LANES), F32)],
        compiler_params=_cparams("parallel", "parallel", "arbitrary"),
        name="rwkv_mix",
    )(three(r_src), three(ld), three(lc), three(km), three(r_src), three(av), three(bv), three(r_src),
      row(r_k), row(ln_g), row(ln_b), s0p)
    return y.reshape(b * t, dm), s_out.reshape(s0.shape)


def _hgrn_lower_bounds(logits):
    p = jax.nn.softmax(logits.astype(F32), axis=0)
    return jnp.maximum(jnp.cumsum(p, axis=0) - p[0], 0.0)


def _mlstm_weights(w_in):
    dq = MLSTM_HEADS * MLSTM_DQK
    o = [0, dq, 2 * dq, 2 * dq + D_MIX, 2 * dq + D_MIX + MLSTM_HEADS, 2 * dq + D_MIX + 2 * MLSTM_HEADS]
    q, k, v = w_in[:, o[0]:o[1]], w_in[:, o[1]:o[2]], w_in[:, o[2]:o[3]]
    gates = w_in[:, o[3]:o[5]]
    rest = w_in[:, o[5]:]
    pad = lambda a: jnp.pad(a.reshape(-1, MLSTM_HEADS, MLSTM_DQK),
                            ((0, 0), (0, 0), (0, MLSTM_DQK_PAD - MLSTM_DQK))).reshape(-1, MLSTM_HEADS * MLSTM_DQK_PAD)
    main = jnp.concatenate([v, rest, pad(q), pad(k)], axis=1).astype(BF16)
    gates = jnp.pad(gates, ((0, 0), (0, LANES - 2 * MLSTM_HEADS))).astype(BF16)
    return main, gates


def _rwkv_weights(w_in, mu):
    c = [0]
    for n in (D_MIX, RWKV_LORA, D_MIX, D_MIX, RWKV_LORA, D_GATE, D_XATTN):
        c.append(c[-1] + n)
    sl = lambda i: w_in[:, c[i]:c[i + 1]]
    main = jnp.concatenate([sl(0), sl(2), sl(3), sl(5), sl(6)], axis=1).astype(BF16)
    padl = lambda a: jnp.pad(a, ((0, 0), (0, RWKV_LORA_PAD - RWKV_LORA)))
    lora = jnp.concatenate([padl(sl(1)), padl(sl(4))], axis=1).astype(BF16)
    mu_ext = jnp.concatenate([mu.astype(F32), jnp.zeros((1, D_MODEL), F32)], axis=0)
    tile = 1024
    main_groups = ([0] * (D_MIX // tile) + [2] * (D_MIX // tile) + [3] * (D_MIX // tile)
                   + [5] * (D_GATE // tile) + [6] * (D_XATTN // tile))
    return main, lora, mu_ext, main_groups, [1, 4]


def _trunk(x3, mk, mv, mem_cols, s_hgrn, s_mc, s_mn, s_mm, s_rwkv, s_shift, prm, hgrn_hb):
    b, t, d = x3.shape
    x = x3.reshape(b * t, d)
    norm_dtype = lambda i: F32 if i % N_MIXERS == 2 else BF16
    out = dict(hgrn=None, mc=[], mn=[], mm=[], rwkv=[], shift=[])
    h = rmsnorm(x, prm['norm_pre'][0], norm_dtype(0))
    for i in range(DEPTH):
        kind, j = i % N_MIXERS, i // N_MIXERS
        if kind == 0:
            p = matmul(h, prm['hgrn_w'], layer=j, tn=512)
            gate_col, q_col = 3 * D_MIX, 3 * D_MIX + D_GATE
            mix, out['hgrn'] = hgrn_mix(p, gate_col, b, t, prm['hgrn_lb'][j], prm['hgrn_norm'][j], s_hgrn, j, hgrn_hb,
                                        out['hgrn'])
        elif kind == 1:
            w_main, w_gates = prm['mlstm_w'][j]
            p = matmul(h, w_main)
            gates = matmul(h, w_gates, tn=LANES)[:, :2 * MLSTM_HEADS]
            gate_col, q_col = 2 * D_MIX, 2 * D_MIX + D_GATE
            mix, c, n, m = mlstm_mix(p, gates, b, t, prm['mlstm_b_gate'][j], prm['mlstm_norm'][j],
                                     s_mc, s_mn, s_mm, j)
            out['mc'].append(c)
            out['mn'].append(n)
            out['mm'].append(m)
        else:
            h3 = h.reshape(b, t, d)
            hp = jnp.concatenate([s_shift[j][:, None, :], h3[:, :-1]], axis=1).reshape(b * t, d)
            w_main, w_lora, mu_ext, main_groups, lora_groups = prm['rwkv_w'][j]
            mixes = token_shift_lerp(h, hp, mu_ext)
            p = matmul_grouped(mixes, w_main, main_groups)
            lora = matmul_grouped(mixes, w_lora, lora_groups, tn=RWKV_LORA_PAD)
            L = math.gcd(t, RWKV_CHUNK)
            ld, lc, km, av, bv = rwkv_prep(p, D_MIX, lora, prm['rwkv_w2'][j], prm['rwkv_a2'][j], prm['rwkv_w0'][j],
                                           prm['rwkv_a0'][j], prm['rwkv_k_k'][j], prm['rwkv_k_a'][j], L)
            gate_col, q_col = 3 * D_MIX, 3 * D_MIX + D_GATE
            mix, s = rwkv_mix(p, 0, 2 * D_MIX, gate_col, ld, lc, km, av, bv, b, t, L, prm['rwkv_r_k'][j],
                              prm['rwkv_ln_g'][j], prm['rwkv_ln_b'][j], s_rwkv[j])
            out['rwkv'].append(s)
            out['shift'].append(h3[:, -1])
        lyr, k_col, v_col = mem_cols(i)
        if isinstance(mk, list):
            xo = memattn(p, q_col, gate_col + D_MIX, mk[i], mv[i], lyr, k_col, v_col, b, t, X_HEADS)
        else:
            xo = memattn_cache(p, q_col, gate_col + D_MIX, mk, mv, lyr, b, t)
        if i + 1 < DEPTH:
            x, h = outproj(mix, xo, prm['w_out'], i, x, prm['norm_post'][i], prm['norm_pre'][i + 1], norm_dtype(i + 1))
        else:
            x, _ = outproj(mix, xo, prm['w_out'], i, x, prm['norm_post'][i])
    cat = lambda parts: parts[0] if len(parts) == 1 else jnp.concatenate(parts, axis=0)
    return (x.reshape(b, t, d), out['hgrn'], cat(out['mc']), cat(out['mn']), cat(out['mm']),
            jnp.stack(out['rwkv']), jnp.stack(out['shift']))


def kernel(x_prompt, x_sample, mem_prompt, cache_mem_k, cache_mem_v, state_hgrn, state_mlstm_c, state_mlstm_n, state_mlstm_m, state_rwkv, state_rwkv_shift, norm_pre, norm_post, norm_mem, w_mem_kv, w_out, hgrn_w_in, hgrn_lb_logits, hgrn_norm, mlstm_w_in, mlstm_b_gate, mlstm_norm, rwkv_w_in, rwkv_mu, rwkv_w0, rwkv_w2, rwkv_a0, rwkv_a2, rwkv_k_k, rwkv_k_a, rwkv_r_k, rwkv_ln_g, rwkv_ln_b):
    n_h, n_m, n_r = hgrn_w_in.shape[0], mlstm_w_in.shape[0], rwkv_w_in.shape[0]
    pad_lora = lambda a: jnp.pad(a, ((0, RWKV_LORA_PAD - RWKV_LORA), (0, 0))).astype(BF16)
    prm = dict(
        norm_pre=norm_pre, norm_post=norm_post, w_out=w_out.astype(BF16),
        hgrn_w=hgrn_w_in,
        hgrn_lb=_hgrn_lower_bounds(hgrn_lb_logits), hgrn_norm=hgrn_norm,
        mlstm_w=[_mlstm_weights(mlstm_w_in[j]) for j in range(n_m)],
        mlstm_b_gate=mlstm_b_gate, mlstm_norm=mlstm_norm,
        rwkv_w=[_rwkv_weights(rwkv_w_in[j], rwkv_mu[j]) for j in range(n_r)],
        rwkv_w0=rwkv_w0, rwkv_w2=[pad_lora(rwkv_w2[j]) for j in range(n_r)],
        rwkv_a0=rwkv_a0, rwkv_a2=[pad_lora(rwkv_a2[j]) for j in range(n_r)],
        rwkv_k_k=rwkv_k_k, rwkv_k_a=rwkv_k_a, rwkv_r_k=rwkv_r_k, rwkv_ln_g=rwkv_ln_g, rwkv_ln_b=rwkv_ln_b)

    bp, n_mem, d = mem_prompt.shape
    mem2 = mem_prompt.reshape(bp * n_mem, d)
    kvs = [matmul(rmsnorm(mem2, norm_mem[i], BF16), w_mem_kv, layer=i, tn=512).reshape(1, bp, n_mem, 2 * D_XATTN)
           for i in range(DEPTH)]
    heads_shape = (bp, n_mem, X_HEADS, X_HEAD_DIM)
    mem_k_prompt = jnp.stack([kv[0, :, :, :D_XATTN].reshape(heads_shape) for kv in kvs])
    mem_v_prompt = jnp.stack([kv[0, :, :, D_XATTN:].reshape(heads_shape) for kv in kvs])

    dt = x_prompt.dtype
    zeros = lambda *shape: jnp.zeros(shape, dt)
    (y_prompt, hgrn_p, mc_p, mn_p, mm_p, rwkv_p, shift_p) = _trunk(
        x_prompt, kvs, kvs, lambda i: (0, 0, D_XATTN),
        zeros(n_h, bp, HGRN_HEADS, HGRN_HEAD_DIM, HGRN_HEAD_DIM),
        zeros(n_m, bp, MLSTM_HEADS, MLSTM_DQK, MLSTM_DV),
        zeros(n_m, bp, MLSTM_HEADS, MLSTM_DQK),
        zeros(n_m, bp, MLSTM_HEADS),
        zeros(n_r, bp, RWKV_HEADS, RWKV_HEAD_DIM, RWKV_HEAD_DIM),
        zeros(n_r, bp, D_MODEL),
        prm, hgrn_hb=6)
    (y_sample, hgrn_s, mc_s, mn_s, mm_s, rwkv_s, shift_s) = _trunk(
        x_sample, cache_mem_k, cache_mem_v, lambda i: (i, 0, 0), state_hgrn, state_mlstm_c, state_mlstm_n, state_mlstm_m,
        state_rwkv, state_rwkv_shift, prm, hgrn_hb=8)
    return (y_prompt, y_sample, mem_k_prompt, mem_v_prompt,
            hgrn_p, mc_p, mn_p, mm_p, rwkv_p, shift_p,
            hgrn_s, mc_s, mn_s, mm_s, rwkv_s, shift_s)
```
